```python
import math
import jax, jax.numpy as jnp
from jax import lax
import numpy as np

D_MODEL = 1024
BATCH = 1
SEQ = 16384
DEPTH = 2
DEC_BATCH = 32
DEC_SEQ = 4
PAST_LEN = 16384
PAGE_SIZE = 128

N_A_LAYERS = DEPTH // 2
N_B_LAYERS = DEPTH - N_A_LAYERS
N_DENSE = (DEPTH + 1) // 2
N_MOE = DEPTH // 2
SSM_GROUP = 16
N_GROUPS = D_MODEL // SSM_GROUP
SSM_STATE = 64
N_HEADS = 16
HEAD_DIM = D_MODEL // N_HEADS
MOBA_BLOCK = 256
MOBA_TOPK = 3
Q_CHUNK = 64
REL_BUCKETS = 32
REL_MAX_DIST = 128
D_FF = 2816
N_EXPERTS = 8
TOP_K = 2
PLE_DIM = 256
EPS = 1e-6
NEG = -1e30

kernel_name = 'yoco_s5_moba_decoder_step'


def _rms(x, g):
    xf = x.astype(jnp.float32)
    return xf * lax.rsqrt(jnp.mean(xf * xf, axis=-1, keepdims=True) + EPS) * g


def _swiglu(x, w1, w3, w2):
    return (jax.nn.silu(x @ w1) * (x @ w3)) @ w2


def _moe(x, router, w1, w3, w2):
    probs = jax.nn.softmax((x @ router).astype(jnp.float32), axis=-1)
    top_p, top_i = lax.top_k(probs, TOP_K)
    top_p = top_p / jnp.sum(top_p, axis=-1, keepdims=True)
    gate = jnp.sum(jax.nn.one_hot(top_i, N_EXPERTS, dtype=top_p.dtype) * top_p[..., None], axis=-2)
    out = None
    for e in range(N_EXPERTS):
        y = gate[..., e:e + 1] * _swiglu(x, w1[e], w3[e], w2[e])
        out = y if out is None else out + y
    return out


def _rel_bucket(dist):
    max_exact = REL_BUCKETS // 2
    d = jnp.maximum(dist, 0)
    large = max_exact + (jnp.log(jnp.maximum(d, max_exact).astype(jnp.float32) / max_exact)
                         / math.log(REL_MAX_DIST / max_exact) * (REL_BUCKETS - max_exact)).astype(jnp.int32)
    large = jnp.minimum(large, REL_BUCKETS - 1)
    return jnp.where(d < max_exact, d, large)


def _s5(u, x0_re, x0_im, lam_re, lam_im, log_dt, b_re, b_im, c_re, c_im, d_skip, w_glu_a, w_glu_b):
    bsz, t_len, _ = u.shape
    uf = u.astype(jnp.float32).reshape(bsz, t_len, N_GROUPS, SSM_GROUP)
    dt = jnp.exp(log_dt.astype(jnp.float32))[:, None]
    lr = lam_re.astype(jnp.float32)
    li = lam_im.astype(jnp.float32)
    mag = jnp.exp(lr * dt)
    a_re = mag * jnp.cos(li * dt)
    a_im = mag * jnp.sin(li * dt)
    den = lr * lr + li * li
    n_re = a_re - 1.0
    k_re = (n_re * lr + a_im * li) / den
    k_im = (a_im * lr - n_re * li) / den
    bb_re = k_re[..., None] * b_re - k_im[..., None] * b_im
    bb_im = k_re[..., None] * b_im + k_im[..., None] * b_re
    bu_re = jnp.einsum('btgc,gpc->btgp', uf, bb_re)
    bu_im = jnp.einsum('btgc,gpc->btgp', uf, bb_im)
    bu_re = bu_re.at[:, 0].add(a_re * x0_re - a_im * x0_im)
    bu_im = bu_im.at[:, 0].add(a_re * x0_im + a_im * x0_re)
    ar = jnp.broadcast_to(a_re, bu_re.shape)
    ai = jnp.broadcast_to(a_im, bu_im.shape)

    def combine(e1, e2):
        a1r, a1i, b1r, b1i = e1
        a2r, a2i, b2r, b2i = e2
        return (a2r * a1r - a2i * a1i, a2r * a1i + a2i * a1r,
                a2r * b1r - a2i * b1i + b2r, a2r * b1i + a2i * b1r + b2i)

    _, _, xr, xi = lax.associative_scan(combine, (ar, ai, bu_re, bu_im), axis=1)
    y = jnp.einsum('btgp,gcp->btgc', xr, c_re) - jnp.einsum('btgp,gcp->btgc', xi, c_im)
    y = y.reshape(bsz, t_len, D_MODEL) + d_skip * u
    h = jax.nn.gelu(y)
    out = (h @ w_glu_a) * jax.nn.sigmoid(h @ w_glu_b)
    return out, xr[:, -1], xi[:, -1]


def _moba_attend(q, k_sel, v_sel, sel_valid, sel_pos, k_own, v_own, own_valid, own_pos, q_pos, rel_bias):
    scale = HEAD_DIM ** -0.5
    qf = q.astype(jnp.float32)
    hh = jnp.arange(N_HEADS)[None, None, :, None]
    b_sel = rel_bias[_rel_bucket(q_pos[None, :, None, None] - sel_pos), hh]
    l_sel = jnp.einsum('bqhd,bqhkd->bqhk', qf, k_sel.astype(jnp.float32)) * scale + b_sel
    l_sel = jnp.where(sel_valid, l_sel, NEG)
    b_own = rel_bias[_rel_bucket(q_pos[:, None] - own_pos[None, :])]
    l_own = jnp.einsum('bqhd,bkhd->bqhk', qf, k_own.astype(jnp.float32)) * scale + b_own.transpose(0, 2, 1)[None]
    l_own = jnp.where(own_valid[None, :, None, :], l_own, NEG)
    ks = l_sel.shape[-1]
    probs = jax.nn.softmax(jnp.concatenate([l_sel, l_own], axis=-1), axis=-1)
    return (jnp.einsum('bqhk,bqhkd->bqhd', probs[..., :ks], v_sel.astype(jnp.float32))
            + jnp.einsum('bqhk,bkhd->bqhd', probs[..., ks:], v_own.astype(jnp.float32)))


def _moba_prompt(q, k, v, rel_bias):
    bsz, s_len, n_h, d_h = q.shape
    nb = -(-s_len // MOBA_BLOCK)
    pad = nb * MOBA_BLOCK - s_len
    kp = jnp.pad(k, ((0, 0), (0, pad), (0, 0), (0, 0))).reshape(bsz, nb, MOBA_BLOCK, n_h, d_h)
    vp = jnp.pad(v, ((0, 0), (0, pad), (0, 0), (0, 0))).reshape(bsz, nb, MOBA_BLOCK, n_h, d_h)
    k_mean = jnp.mean(kp.astype(jnp.float32), axis=2)
    n_cols = max(nb, MOBA_TOPK)
    bb = jnp.arange(bsz)[:, None, None, None, None]
    hh = jnp.arange(n_h)[None, None, :, None, None]
    rr = jnp.arange(MOBA_BLOCK)

    def chunk(c):
        q0 = c * Q_CHUNK
        qc = lax.dynamic_slice_in_dim(q, q0, Q_CHUNK, axis=1)
        own = q0 // MOBA_BLOCK
        q_pos = q0 + jnp.arange(Q_CHUNK)
        scores = jnp.einsum('bqhd,bnhd->bqhn', qc.astype(jnp.float32), k_mean)
        scores = jnp.where(jnp.arange(nb) < own, scores, NEG)
        scores = jnp.pad(scores, ((0, 0), (0, 0), (0, 0), (0, n_cols - nb)), constant_values=NEG)
        _, idx = lax.top_k(scores, MOBA_TOPK)
        valid = idx < own
        idx = jnp.minimum(idx, nb - 1)
        k_sel = kp[bb, idx[..., None], rr, hh].reshape(bsz, Q_CHUNK, n_h, MOBA_TOPK * MOBA_BLOCK, d_h)
        v_sel = vp[bb, idx[..., None], rr, hh].reshape(bsz, Q_CHUNK, n_h, MOBA_TOPK * MOBA_BLOCK, d_h)
        sel_pos = (idx[..., None] * MOBA_BLOCK + rr).reshape(bsz, Q_CHUNK, n_h, -1)
        sel_valid = jnp.broadcast_to(valid[..., None], valid.shape + (MOBA_BLOCK,)).reshape(bsz, Q_CHUNK, n_h, -1)
        k_own = lax.dynamic_slice_in_dim(kp, own, 1, axis=1)[:, 0]
        v_own = lax.dynamic_slice_in_dim(vp, own, 1, axis=1)[:, 0]
        own_pos = own * MOBA_BLOCK + rr
        own_valid = own_pos[None, :] <= q_pos[:, None]
        return _moba_attend(qc, k_sel, v_sel, sel_valid, sel_pos, k_own, v_own, own_valid, own_pos, q_pos, rel_bias)

    out = lax.map(chunk, jnp.arange(s_len // Q_CHUNK))
    return out.transpose(1, 0, 2, 3, 4).reshape(bsz, s_len, n_h, d_h)


def _moba_sample(q, k_new, v_new, cache_k, cache_v, page_table, rel_bias):
    n_db, t_len, n_h, d_h = q.shape
    n_pages = page_table.shape[1]
    past_len = n_pages * PAGE_SIZE
    ppb = MOBA_BLOCK // PAGE_SIZE
    block_start = (past_len // MOBA_BLOCK) * MOBA_BLOCK
    n_past = block_start // MOBA_BLOCK
    n_tail = (past_len - block_start) // PAGE_SIZE
    q_pos = past_len + jnp.arange(t_len)
    page_mean = jnp.mean(cache_k.astype(jnp.float32), axis=1)
    pm = page_mean[page_table[:, :n_past * ppb]].reshape(n_db, n_past, ppb, n_h, d_h).mean(axis=2)
    scores = jnp.einsum('bqhd,bnhd->bqhn', q.astype(jnp.float32), pm)
    n_cols = max(n_past, MOBA_TOPK)
    scores = jnp.pad(scores, ((0, 0), (0, 0), (0, 0), (0, n_cols - n_past)), constant_values=NEG)
    _, idx = lax.top_k(scores, MOBA_TOPK)
    valid = idx < n_past
    idx = jnp.clip(idx, 0, max(n_past - 1, 0))
    logical = jnp.minimum(idx[..., None] * ppb + jnp.arange(ppb), n_pages - 1)
    phys = page_table[jnp.arange(n_db)[:, None, None, None, None], logical]
    rr = jnp.arange(PAGE_SIZE)
    hh = jnp.arange(n_h)[None, None, :, None, None, None]
    n_sel = MOBA_TOPK * MOBA_BLOCK
    k_sel = cache_k[phys[..., None], rr, hh].reshape(n_db, t_len, n_h, n_sel, d_h)
    v_sel = cache_v[phys[..., None], rr, hh].reshape(n_db, t_len, n_h, n_sel, d_h)
    sel_pos = (logical[..., None] * PAGE_SIZE + rr).reshape(n_db, t_len, n_h, n_sel)
    sel_valid = jnp.broadcast_to(valid[..., None, None], valid.shape + (ppb, PAGE_SIZE)).reshape(n_db, t_len, n_h, n_sel)
    tail_phys = page_table[:, n_pages - n_tail:]
    k_tail = cache_k[tail_phys].reshape(n_db, n_tail * PAGE_SIZE, n_h, d_h)
    v_tail = cache_v[tail_phys].reshape(n_db, n_tail * PAGE_SIZE, n_h, d_h)
    k_own = jnp.concatenate([k_tail.astype(jnp.float32), k_new.astype(jnp.float32)], axis=1)
    v_own = jnp.concatenate([v_tail.astype(jnp.float32), v_new.astype(jnp.float32)], axis=1)
    own_pos = jnp.concatenate([block_start + jnp.arange(n_tail * PAGE_SIZE), q_pos])
    own_valid = own_pos[None, :] <= q_pos[:, None]
    return _moba_attend(q, k_sel, v_sel, sel_valid, sel_pos, k_own, v_own, own_valid, own_pos, q_pos, rel_bias)


def setup_inputs(seed: int = 0) -> dict:
    key = jax.random.key(seed)
    ks = iter(jax.random.split(key, 48))
    f32 = jnp.float32

    def nrm(shape, scale):
        return jax.random.normal(next(ks), shape, f32) * scale

    n_pages = PAST_LEN // PAGE_SIZE
    n_pool = (DEC_BATCH * n_pages * 5 + 3) // 4
    perm = jax.random.permutation(next(ks), n_pool)
    page_table = perm[:DEC_BATCH * n_pages].reshape(DEC_BATCH, n_pages).astype(jnp.int32)
    lam_im0 = jnp.broadcast_to(jnp.pi * jnp.arange(SSM_STATE, dtype=f32), (N_A_LAYERS, N_GROUPS, SSM_STATE))
    d_in = D_MODEL ** -0.5
    return {
        'x_prompt': nrm((BATCH, SEQ, D_MODEL), 1.0),
        'x_sample': nrm((DEC_BATCH, DEC_SEQ, D_MODEL), 1.0),
        'cache_k': nrm((n_pool, PAGE_SIZE, N_HEADS, HEAD_DIM), 1.0),
        'cache_v': nrm((n_pool, PAGE_SIZE, N_HEADS, HEAD_DIM), 1.0),
        'state_ssm_re': nrm((N_A_LAYERS, DEC_BATCH, N_GROUPS, SSM_STATE), 0.1),
        'state_ssm_im': nrm((N_A_LAYERS, DEC_BATCH, N_GROUPS, SSM_STATE), 0.1),
        'page_table': page_table,
        'p_prompt': nrm((DEPTH, BATCH, SEQ, PLE_DIM), 1.0),
        'p_sample': nrm((DEPTH, DEC_BATCH, DEC_SEQ, PLE_DIM), 1.0),
        'rel_bias': nrm((REL_BUCKETS, N_HEADS), 0.5),
        'norm_mix': 1.0 + nrm((DEPTH, D_MODEL), 0.02),
        'norm_ffn': 1.0 + nrm((DEPTH, D_MODEL), 0.02),
        'norm_kv': 1.0 + nrm((D_MODEL,), 0.02),
        'norm_final': 1.0 + nrm((D_MODEL,), 0.02),
        'ssm_lam_re': -0.5 + nrm((N_A_LAYERS, N_GROUPS, SSM_STATE), 0.01),
        'ssm_lam_im': lam_im0 + nrm((N_A_LAYERS, N_GROUPS, SSM_STATE), 0.01),
        'ssm_log_dt': jax.random.uniform(next(ks), (N_A_LAYERS, N_GROUPS), f32, math.log(1e-3), math.log(1e-1)),
        'ssm_b_re': nrm((N_A_LAYERS, N_GROUPS, SSM_STATE, SSM_GROUP), (2 * SSM_GROUP) ** -0.5),
        'ssm_b_im': nrm((N_A_LAYERS, N_GROUPS, SSM_STATE, SSM_GROUP), (2 * SSM_GROUP) ** -0.5),
        'ssm_c_re': nrm((N_A_LAYERS, N_GROUPS, SSM_GROUP, SSM_STATE), (2 * SSM_STATE) ** -0.5),
        'ssm_c_im': nrm((N_A_LAYERS, N_GROUPS, SSM_GROUP, SSM_STATE), (2 * SSM_STATE) ** -0.5),
        'ssm_d': nrm((N_A_LAYERS, D_MODEL), 1.0),
        'w_glu_a': nrm((N_A_LAYERS, D_MODEL, D_MODEL), d_in),
        'w_glu_b': nrm((N_A_LAYERS, D_MODEL, D_MODEL), d_in),
        'w_q': nrm((N_B_LAYERS, D_MODEL, D_MODEL), d_in),
        'w_k': nrm((D_MODEL, D_MODEL), d_in),
        'w_v': nrm((D_MODEL, D_MODEL), d_in),
        'w_o': nrm((N_B_LAYERS, D_MODEL, D_MODEL), d_in),
        'ffn_w1': nrm((N_DENSE, D_MODEL, D_FF), d_in),
        'ffn_w3': nrm((N_DENSE, D_MODEL, D_FF), d_in),
        'ffn_w2': nrm((N_DENSE, D_FF, D_MODEL), D_FF ** -0.5),
        'router': nrm((N_MOE, D_MODEL, N_EXPERTS), d_in),
        'moe_w1': nrm((N_MOE, N_EXPERTS, D_MODEL, D_FF), d_in),
        'moe_w3': nrm((N_MOE, N_EXPERTS, D_MODEL, D_FF), d_in),
        'moe_w2': nrm((N_MOE, N_EXPERTS, D_FF, D_MODEL), D_FF ** -0.5),
        'w_ple': nrm((DEPTH, PLE_DIM, D_MODEL), PLE_DIM ** -0.5),
        'w_ple_gate': nrm((DEPTH, D_MODEL, D_MODEL), d_in),
    }


def reference(x_prompt, x_sample, cache_k, cache_v, state_ssm_re, state_ssm_im, page_table,
              p_prompt, p_sample, rel_bias, norm_mix, norm_ffn, norm_kv, norm_final,
              ssm_lam_re, ssm_lam_im, ssm_log_dt, ssm_b_re, ssm_b_im, ssm_c_re, ssm_c_im, ssm_d,
              w_glu_a, w_glu_b, w_q, w_k, w_v, w_o, ffn_w1, ffn_w3, ffn_w2,
              router, moe_w1, moe_w3, moe_w2, w_ple, w_ple_gate):

    def trunk(x, p, s0_re, s0_im, attend):
        bsz, t_len, _ = x.shape
        h = x.astype(jnp.float32)
        new_re, new_im = [], []
        k = None
        v = None
        for i in range(DEPTH):
            hn = _rms(h, norm_mix[i])
            if i < N_A_LAYERS:
                out, sr, si = _s5(hn, s0_re[i], s0_im[i], ssm_lam_re[i], ssm_lam_im[i], ssm_log_dt[i],
                                  ssm_b_re[i], ssm_b_im[i], ssm_c_re[i], ssm_c_im[i], ssm_d[i],
                                  w_glu_a[i], w_glu_b[i])
                new_re.append(sr)
                new_im.append(si)
            else:
                j = i - N_A_LAYERS
                if j == 0:
                    kv_in = _rms(h, norm_kv)
                    k = (kv_in @ w_k).reshape(bsz, t_len, N_HEADS, HEAD_DIM)
                    v = (kv_in @ w_v).reshape(bsz, t_len, N_HEADS, HEAD_DIM)
                q = (hn @ w_q[j]).reshape(bsz, t_len, N_HEADS, HEAD_DIM)
                out = attend(q, k, v).reshape(bsz, t_len, D_MODEL) @ w_o[j]
            h = h + out
            hn = _rms(h, norm_ffn[i])
            if i % 2 == 0:
                out = _swiglu(hn, ffn_w1[i // 2], ffn_w3[i // 2], ffn_w2[i // 2])
            else:
                out = _moe(hn, router[i // 2], moe_w1[i // 2], moe_w3[i // 2], moe_w2[i // 2])
            h = h + out
            h = h + (p[i] @ w_ple[i]) * jax.nn.sigmoid(h @ w_ple_gate[i])
        return _rms(h, norm_final), k, v, jnp.stack(new_re), jnp.stack(new_im)

    zeros_state = jnp.zeros((N_A_LAYERS, x_prompt.shape[0], N_GROUPS, SSM_STATE), jnp.float32)
    y_prompt, k_prompt, v_prompt, ssm_re_prompt, ssm_im_prompt = trunk(
        x_prompt, p_prompt, zeros_state, zeros_state,
        lambda q, k, v: _moba_prompt(q, k, v, rel_bias))
    y_sample, k_sample, v_sample, ssm_re_sample, ssm_im_sample = trunk(
        x_sample, p_sample, state_ssm_re, state_ssm_im,
        lambda q, k, v: _moba_sample(q, k, v, cache_k, cache_v, page_table, rel_bias))
    return (y_prompt, y_sample, k_prompt, v_prompt, ssm_re_prompt, ssm_im_prompt,
            k_sample, v_sample, ssm_re_sample, ssm_im_sample)
```

```python
import functools
import math

import jax
import jax.numpy as jnp
import numpy as np
from jax import lax
from jax.experimental import pallas as pl
from jax.experimental.pallas import tpu as pltpu

F32 = jnp.float32
BF16 = jnp.bfloat16

D_MODEL = 1024
SSM_GROUP = 16
N_GROUPS = D_MODEL // SSM_GROUP
SSM_STATE = 64
N_STATE = N_GROUPS * SSM_STATE
S5_BLOCKS = 4
S5_IN = D_MODEL // S5_BLOCKS
S5_ST = N_STATE // S5_BLOCKS
N_HEADS = 16
HEAD_DIM = 64
MOBA_BLOCK = 256
MOBA_TOPK = 3
PAGE_SIZE = 128
PAGES_PER_BLOCK = MOBA_BLOCK // PAGE_SIZE
REL_BUCKETS = 32
REL_MAX_DIST = 128
D_FF = 2816
FF_HALF = D_FF // 2
N_EXPERTS = 8
PLE_DIM = 256
EPS = 1e-6
NEG = -1e30
SUBLANES = 8
LANES = 128
VMEM_LIMIT = 56 * 1024 * 1024
PAGES_PER_STEP = 16


def _rmsn(x):
    return x * lax.rsqrt(jnp.mean(x * x, axis=-1, keepdims=True) + EPS)


def _dot(a, b):
    return jnp.dot(a, b, preferred_element_type=F32)


def _dot_nt(a, b):
    return lax.dot_general(a, b, (((1,), (1,)), ((), ())), preferred_element_type=F32)


def _split2(x):
    hi = x.astype(BF16)
    lo = (x - hi.astype(F32)).astype(BF16)
    return hi, lo


def _const_spec(shape):
    nd = len(shape)
    return pl.BlockSpec(shape, lambda *_: (0,) * nd, pipeline_mode=pl.Buffered(1))


def _params(sem, vmem=VMEM_LIMIT):
    return pltpu.CompilerParams(dimension_semantics=sem, vmem_limit_bytes=vmem)


def _s5_tables(lam_re, lam_im, log_dt, b_re, b_im, c_re, c_im):
    dt = jnp.exp(log_dt.astype(F32))[:, None]
    lr = lam_re.astype(F32)
    li = lam_im.astype(F32)
    mag = jnp.exp(lr * dt)
    a_re = mag * jnp.cos(li * dt)
    a_im = mag * jnp.sin(li * dt)
    den = lr * lr + li * li
    n_re = a_re - 1.0
    k_re = (n_re * lr + a_im * li) / den
    k_im = (a_im * lr - n_re * li) / den
    bb_re = k_re[..., None] * b_re - k_im[..., None] * b_im
    bb_im = k_re[..., None] * b_im + k_im[..., None] * b_re
    gpb = N_GROUPS // S5_BLOCKS
    eye = jnp.eye(gpb, dtype=F32)

    def pack_b(bb):
        t = bb.reshape(S5_BLOCKS, gpb, SSM_STATE, SSM_GROUP)
        t = jnp.einsum('jgpc,gh->jgchp', t, eye)
        return t.reshape(S5_BLOCKS, S5_IN, S5_ST)

    def pack_c(cc):
        t = cc.astype(F32).reshape(S5_BLOCKS, gpb, SSM_GROUP, SSM_STATE)
        t = jnp.einsum('jgcp,gh->jgphc', t, eye)
        return t.reshape(S5_BLOCKS, S5_ST, S5_IN)

    bblk = jnp.concatenate([pack_b(bb_re), pack_b(bb_im)], axis=-1).astype(BF16)
    cre = pack_c(c_re).astype(BF16)
    cim = (-pack_c(c_im)).astype(BF16)
    ar = a_re.reshape(1, N_STATE)
    ai = a_im.reshape(1, N_STATE)
    pw = [(ar, ai)]
    for _ in range(SUBLANES - 1):
        pr, pi = pw[-1]
        pw.append((pr * ar - pi * ai, pr * ai + pi * ar))
    row = jnp.arange(SUBLANES)[:, None]

    def masked(k):
        return (jnp.where(row >= k, pw[k - 1][0], 0.0), jnp.where(row >= k, pw[k - 1][1], 0.0))

    m1, m2, m4 = masked(1), masked(2), masked(4)
    a_r = jnp.concatenate([p[0] for p in pw], axis=0)
    a_i = jnp.concatenate([p[1] for p in pw], axis=0)
    tabs = jnp.stack([m1[0], m1[1], m2[0], m2[1], m4[0], m4[1], a_r, a_i])
    a1 = jnp.concatenate([ar, ai], axis=0)
    return bblk, cre, cim, tabs, a1


def _s5_out(x, u, y, d_ref, wa_ref, wb_ref):
    y = y + d_ref[...] * u
    hb = jax.nn.gelu(y).astype(BF16)
    return x + _dot(hb, wa_ref[...]) * jax.nn.sigmoid(_dot(hb, wb_ref[...]))


def _s5_prompt_kernel(x_ref, g_ref, bblk_ref, cre_ref, cim_ref, tabs_ref, d_ref, wa_ref, wb_ref,
                      h_ref, sre_ref, sim_ref, cr_s, ci_s, xr_s, xi_s):
    @pl.when(pl.program_id(0) == 0)
    def _():
        cr_s[...] = jnp.zeros_like(cr_s)
        ci_s[...] = jnp.zeros_like(ci_s)

    x = x_ref[...]
    u = _rmsn(x) * g_ref[...]
    ub = u.astype(BF16)
    n8 = x.shape[0] // SUBLANES
    ys = []
    for j in range(S5_BLOCKS):
        lo, hi = j * S5_ST, (j + 1) * S5_ST
        bu = _dot(ub[:, j * S5_IN:(j + 1) * S5_IN], bblk_ref[j])
        xr = bu[:, :S5_ST].reshape(n8, SUBLANES, S5_ST)
        xi = bu[:, S5_ST:].reshape(n8, SUBLANES, S5_ST)
        for k, sh in ((0, 1), (2, 2), (4, 4)):
            mr = tabs_ref[k, :, lo:hi]
            mi = tabs_ref[k + 1, :, lo:hi]
            rr = pltpu.roll(xr, sh, 1)
            ri = pltpu.roll(xi, sh, 1)
            xr, xi = xr + (mr * rr - mi * ri), xi + (mr * ri + mi * rr)
        a_r = tabs_ref[6, :, lo:hi]
        a_i = tabs_ref[7, :, lo:hi]
        pr = cr_s[:, lo:hi]
        pi = ci_s[:, lo:hi]
        for t in range(n8):
            br = jnp.broadcast_to(pr, (SUBLANES, S5_ST))
            bi = jnp.broadcast_to(pi, (SUBLANES, S5_ST))
            tr = xr[t] + (a_r * br - a_i * bi)
            ti = xi[t] + (a_r * bi + a_i * br)
            xr_s[t * SUBLANES:(t + 1) * SUBLANES, :] = tr
            xi_s[t * SUBLANES:(t + 1) * SUBLANES, :] = ti
            pr = tr[SUBLANES - 1:SUBLANES, :]
            pi = ti[SUBLANES - 1:SUBLANES, :]
        cr_s[:, lo:hi] = pr
        ci_s[:, lo:hi] = pi
        ys.append(_dot(xr_s[...].astype(BF16), cre_ref[j]) + _dot(xi_s[...].astype(BF16), cim_ref[j]))
    y = jnp.concatenate(ys, axis=1)
    h_ref[...] = _s5_out(x, u, y, d_ref, wa_ref, wb_ref)
    sre_ref[...] = cr_s[...]
    sim_ref[...] = ci_s[...]


def _s5_prompt(x, g, bblk, cre, cim, tabs, d, wa, wb, chunk):
    t_len = x.shape[0]
    chunk = min(chunk, t_len)
    return pl.pallas_call(
        _s5_prompt_kernel,
        grid=(t_len // chunk,),
        in_specs=[pl.BlockSpec((chunk, D_MODEL), lambda i: (i, 0)),
                  _const_spec(g.shape), _const_spec(bblk.shape), _const_spec(cre.shape), _const_spec(cim.shape),
                  _const_spec(tabs.shape), _const_spec(d.shape), _const_spec(wa.shape), _const_spec(wb.shape)],
        out_specs=[pl.BlockSpec((chunk, D_MODEL), lambda i: (i, 0)),
                   pl.BlockSpec((1, N_STATE), lambda i: (0, 0)),
                   pl.BlockSpec((1, N_STATE), lambda i: (0, 0))],
        out_shape=[jax.ShapeDtypeStruct((t_len, D_MODEL), F32),
                   jax.ShapeDtypeStruct((1, N_STATE), F32),
                   jax.ShapeDtypeStruct((1, N_STATE), F32)],
        scratch_shapes=[pltpu.VMEM((1, N_STATE), F32), pltpu.VMEM((1, N_STATE), F32),
                        pltpu.VMEM((chunk, S5_ST), F32), pltpu.VMEM((chunk, S5_ST), F32)],
        compiler_params=_params(("arbitrary",)),
        name="s5_prompt",
    )(x, g, bblk, cre, cim, tabs, d, wa, wb)


def _s5_sample_kernel(x_ref, g_ref, bblk_ref, cre_ref, cim_ref, a_ref, s0r_ref, s0i_ref, d_ref, wa_ref, wb_ref,
                      h_ref, sre_ref, sim_ref, xr_s, xi_s, *, n_seq, n_tok):
    x = x_ref[...]
    u = _rmsn(x) * g_ref[...]
    ub = u.astype(BF16)
    ys = []
    for j in range(S5_BLOCKS):
        lo, hi = j * S5_ST, (j + 1) * S5_ST
        bu = _dot(ub[:, j * S5_IN:(j + 1) * S5_IN], bblk_ref[j])
        ar = a_ref[0:1, lo:hi]
        ai = a_ref[1:2, lo:hi]
        sr = s0r_ref[:, lo:hi]
        si = s0i_ref[:, lo:hi]
        for t in range(n_tok):
            rows = bu[t * n_seq:(t + 1) * n_seq]
            sr, si = ar * sr - ai * si + rows[:, :S5_ST], ar * si + ai * sr + rows[:, S5_ST:]
            xr_s[t * n_seq:(t + 1) * n_seq, :] = sr
            xi_s[t * n_seq:(t + 1) * n_seq, :] = si
        sre_ref[:, lo:hi] = sr
        sim_ref[:, lo:hi] = si
        ys.append(_dot(xr_s[...].astype(BF16), cre_ref[j]) + _dot(xi_s[...].astype(BF16), cim_ref[j]))
    y = jnp.concatenate(ys, axis=1)
    h_ref[...] = _s5_out(x, u, y, d_ref, wa_ref, wb_ref)


def _s5_sample(x_tm, g, bblk, cre, cim, a1, s0r, s0i, d, wa, wb, n_seq, n_tok):
    n = n_seq * n_tok
    args = (x_tm, g, bblk, cre, cim, a1, s0r, s0i, d, wa, wb)
    return pl.pallas_call(
        functools.partial(_s5_sample_kernel, n_seq=n_seq, n_tok=n_tok),
        grid=(1,),
        in_specs=[_const_spec(a.shape) for a in args],
        out_specs=[_const_spec((n, D_MODEL)), _const_spec((n_seq, N_STATE)), _const_spec((n_seq, N_STATE))],
        out_shape=[jax.ShapeDtypeStruct((n, D_MODEL), F32),
                   jax.ShapeDtypeStruct((n_seq, N_STATE), F32),
                   jax.ShapeDtypeStruct((n_seq, N_STATE), F32)],
        scratch_shapes=[pltpu.VMEM((n, S5_ST), F32), pltpu.VMEM((n, S5_ST), F32)],
        compiler_params=_params(("arbitrary",)),
        name="s5_sample",
    )(*args)


def _ffn0_kernel(h1_ref, p_ref, gf_ref, w1_ref, w3_ref, w2_ref, wple_ref, wgate_ref, gkv_ref, gq_ref,
                 wk_ref, wv_ref, wq_ref, h3_ref, k_ref, v_ref, q_ref, *km_ref, n_blk):
    x = h1_ref[...]
    hn = (_rmsn(x) * gf_ref[...]).astype(BF16)
    h2 = x
    for c in range(2):
        lo, hi = c * FF_HALF, (c + 1) * FF_HALF
        a = _dot(hn, w1_ref[:, lo:hi])
        b = _dot(hn, w3_ref[:, lo:hi])
        h2 = h2 + _dot((jax.nn.silu(a) * b).astype(BF16), w2_ref[lo:hi, :])
    ple = _dot(p_ref[...].astype(BF16), wple_ref[...]) * jax.nn.sigmoid(_dot(h2.astype(BF16), wgate_ref[...]))
    h3 = h2 + ple
    h3_ref[...] = h3
    xn = _rmsn(h3)
    kvb = (xn * gkv_ref[...]).astype(BF16)
    qb = (xn * gq_ref[...]).astype(BF16)
    k = _dot(kvb, wk_ref[...])
    k_ref[...] = k
    v_ref[...] = _dot(kvb, wv_ref[...])
    q_ref[...] = _dot(qb, wq_ref[...])
    for b in range(n_blk):
        km_ref[0][b] = jnp.mean(k[b * MOBA_BLOCK:(b + 1) * MOBA_BLOCK], axis=0, keepdims=True)


def _ffn0(h1, p, gf, w1, w3, w2, wple, wgate, gkv, gq, wk, wv, wq, tm, with_kmean):
    n = h1.shape[0]
    tm = min(tm, n)
    n_blk = tm // MOBA_BLOCK if with_kmean else 0
    consts = (gf, w1, w3, w2, wple, wgate, gkv, gq, wk, wv, wq)
    tile = pl.BlockSpec((tm, D_MODEL), lambda i: (i, 0))
    out_specs = [tile, tile, tile, tile]
    out_shape = [jax.ShapeDtypeStruct((n, D_MODEL), F32)] * 4
    if with_kmean:
        out_specs.append(pl.BlockSpec((n_blk, 1, D_MODEL), lambda i: (i, 0, 0)))
        out_shape.append(jax.ShapeDtypeStruct((n // MOBA_BLOCK, 1, D_MODEL), F32))
    return pl.pallas_call(
        functools.partial(_ffn0_kernel, n_blk=n_blk),
        grid=(n // tm,),
        in_specs=[tile, pl.BlockSpec((tm, PLE_DIM), lambda i: (i, 0))] + [_const_spec(c.shape) for c in consts],
        out_specs=out_specs,
        out_shape=out_shape,
        compiler_params=_params(("arbitrary",)),
        name="ffn0_qkv",
    )(h1, p, *consts)


def _rel_bucket(dist):
    max_exact = REL_BUCKETS // 2
    d = jnp.maximum(dist, 0)
    large = max_exact + (jnp.log(jnp.maximum(d, max_exact).astype(F32) / max_exact)
                         / math.log(REL_MAX_DIST / max_exact) * (REL_BUCKETS - max_exact)).astype(jnp.int32)
    large = jnp.minimum(large, REL_BUCKETS - 1)
    return jnp.where(d < max_exact, d, large)


def _bias_by_distance(rel_bias, n):
    return rel_bias[_rel_bucket(jnp.arange(n))]


FAR_DIST = REL_MAX_DIST


def _top3_rows(cur, rows, n_rows):
    picked = jnp.zeros(cur.shape, jnp.bool_)
    for _ in range(MOBA_TOPK):
        m = jnp.max(cur, axis=0, keepdims=True)
        idx = jnp.min(jnp.where(cur == m, rows, n_rows), axis=0, keepdims=True)
        hit = rows == idx
        picked = jnp.logical_or(picked, hit)
        cur = jnp.where(hit, -jnp.inf, cur)
    return picked


def _moba_prompt_kernel(q_ref, k_ref, v_ref, km_ref, bias_ref, o_ref, sel_s, *, n_blk):
    own = pl.program_id(1)
    q = q_ref[0]
    tq = q.shape[0]
    q_hi, q_lo = _split2(q)
    km_hi, km_lo = _split2(km_ref[0])
    sc = _dot_nt(km_hi, q_hi) + (_dot_nt(km_hi, q_lo) + _dot_nt(km_lo, q_hi))
    rows = lax.broadcasted_iota(jnp.int32, sc.shape, 0)
    past = rows < own
    picked = _top3_rows(jnp.where(past, sc, NEG), rows, n_blk)
    sel = jnp.where(jnp.logical_and(picked, past), 1.0, 0.0)
    for n in range(n_blk):
        sel_s[n] = sel[n:n + 1, :]

    qb = (q * (HEAD_DIM ** -0.5)).astype(BF16)

    def step(n, carry, bias):
        m_run, l_run, acc = carry
        s = _dot_nt(k_ref[0, n], qb)
        if bias is not None:
            s = s + bias
        on = sel_s[n] > 0.0
        mb = jnp.where(on, jnp.max(s, axis=0, keepdims=True), NEG)
        m_new = jnp.maximum(m_run, mb)
        alpha = jnp.exp(m_run - m_new)
        p = jnp.exp(s - jnp.where(on, m_new, -NEG))
        l_new = alpha * l_run + jnp.sum(p, axis=0, keepdims=True)
        acc = alpha * acc + _dot(v_ref[0, n], p.astype(BF16))
        return m_new, l_new, acc

    init = (jnp.full((1, tq), NEG, F32), jnp.zeros((1, tq), F32), jnp.zeros((HEAD_DIM, tq), F32))
    carry = lax.fori_loop(0, jnp.maximum(own - 1, 0), lambda n, c: step(n, c, None), init)
    carry = step(jnp.maximum(own - 1, 0), carry, bias_ref[0, 1])
    m_run, l_run, acc = carry
    s = _dot_nt(k_ref[0, own], qb) + bias_ref[0, 0]
    m_new = jnp.maximum(m_run, jnp.max(s, axis=0, keepdims=True))
    alpha = jnp.exp(m_run - m_new)
    p = jnp.exp(s - m_new)
    l_new = alpha * l_run + jnp.sum(p, axis=0, keepdims=True)
    acc = alpha * acc + _dot(v_ref[0, own], p.astype(BF16))
    o_ref[0] = acc / l_new


def _moba_prompt(q, k, v, kmean, rel_bias):
    t_len = q.shape[0]
    n_blk = t_len // MOBA_BLOCK
    qh = q.reshape(t_len, N_HEADS, HEAD_DIM).transpose(1, 0, 2)
    kh = k.astype(BF16).reshape(n_blk, MOBA_BLOCK, N_HEADS, HEAD_DIM).transpose(2, 0, 1, 3)
    vt = v.astype(BF16).reshape(n_blk, MOBA_BLOCK, N_HEADS, HEAD_DIM).transpose(2, 0, 3, 1)
    kmh = kmean.reshape(n_blk, N_HEADS, HEAD_DIM).transpose(1, 0, 2)
    bd = _bias_by_distance(rel_bias, 2 * MOBA_BLOCK)
    far = rel_bias[REL_BUCKETS - 1]
    dmat = jnp.arange(MOBA_BLOCK)[None, :] - jnp.arange(MOBA_BLOCK)[:, None]
    own_t = jnp.where((dmat >= 0)[..., None], bd[jnp.maximum(dmat, 0)] - far, NEG)
    prev_t = bd[dmat + MOBA_BLOCK] - far
    bias = jnp.stack([own_t, prev_t]).transpose(3, 0, 1, 2)
    out = pl.pallas_call(
        functools.partial(_moba_prompt_kernel, n_blk=n_blk),
        grid=(N_HEADS, n_blk),
        in_specs=[pl.BlockSpec((1, MOBA_BLOCK, HEAD_DIM), lambda h, i: (h, i, 0)),
                  pl.BlockSpec((1, n_blk, MOBA_BLOCK, HEAD_DIM), lambda h, i: (h, 0, 0, 0)),
                  pl.BlockSpec((1, n_blk, HEAD_DIM, MOBA_BLOCK), lambda h, i: (h, 0, 0, 0)),
                  pl.BlockSpec((1, n_blk, HEAD_DIM), lambda h, i: (h, 0, 0)),
                  pl.BlockSpec((1, 2, MOBA_BLOCK, MOBA_BLOCK), lambda h, i: (h, 0, 0, 0))],
        out_specs=pl.BlockSpec((1, HEAD_DIM, MOBA_BLOCK), lambda h, i: (h, 0, i)),
        out_shape=jax.ShapeDtypeStruct((N_HEADS, HEAD_DIM, t_len), F32),
        scratch_shapes=[pltpu.VMEM((n_blk, 1, MOBA_BLOCK), F32)],
        compiler_params=_params(("arbitrary", "arbitrary")),
        name="moba_prompt",
    )(qh, kh, vt, kmh, bias)
    return out.transpose(2, 0, 1).reshape(t_len, D_MODEL)


def _page_mean_kernel(pt_ref, *refs):
    del pt_ref
    o_ref = refs[-1]
    for b in range(PAGES_PER_STEP // PAGES_PER_BLOCK):
        s = jnp.sum(refs[2 * b][0], axis=0, keepdims=True) + jnp.sum(refs[2 * b + 1][0], axis=0, keepdims=True)
        o_ref[b:b + 1, :] = s * (1.0 / MOBA_BLOCK)


def _block_means(cache_k3, pt_flat):
    n_pages_total = pt_flat.shape[0]
    steps = n_pages_total // PAGES_PER_STEP
    blocks_per_step = PAGES_PER_STEP // PAGES_PER_BLOCK

    def page_spec(r):
        return pl.BlockSpec((1, PAGE_SIZE, D_MODEL), lambda i, pt: (pt[i * PAGES_PER_STEP + r], 0, 0))

    return pl.pallas_call(
        _page_mean_kernel,
        grid_spec=pltpu.PrefetchScalarGridSpec(
            num_scalar_prefetch=1, grid=(steps,),
            in_specs=[page_spec(r) for r in range(PAGES_PER_STEP)],
            out_specs=pl.BlockSpec((blocks_per_step, D_MODEL), lambda i, pt: (i, 0))),
        out_shape=jax.ShapeDtypeStruct((n_pages_total // PAGES_PER_BLOCK, D_MODEL), F32),
        compiler_params=_params(("arbitrary",)),
        name="page_means",
    )(pt_flat, *([cache_k3] * PAGES_PER_STEP))


def _sample_select_kernel(q_ref, pm_ref, hind_ref, idx_ref, *, n_tok, n_past):
    pm = pm_ref[...]
    rows = lax.broadcasted_iota(jnp.int32, (n_past, LANES), 0)
    for t in range(n_tok):
        prod = pm * q_ref[0, t:t + 1, :]
        p1 = prod.astype(BF16)
        r1 = prod - p1.astype(F32)
        p2 = r1.astype(BF16)
        p3 = (r1 - p2.astype(F32)).astype(BF16)
        cur = _dot(p1, hind_ref[...]) + (_dot(p2, hind_ref[...]) + _dot(p3, hind_ref[...]))
        for i in range(MOBA_TOPK):
            m = jnp.max(cur, axis=0, keepdims=True)
            idx = jnp.min(jnp.where(cur == m, rows, n_past), axis=0, keepdims=True)
            idx_ref[0, t, i:i + 1, :] = idx
            cur = jnp.where(rows == idx, -jnp.inf, cur)


def _sample_select(q3, pm, n_past):
    n_seq, n_tok, _ = q3.shape
    head_of_lane = jnp.arange(D_MODEL)[:, None] // HEAD_DIM
    hind = (head_of_lane == jnp.arange(LANES)[None, :]).astype(BF16)
    return pl.pallas_call(
        functools.partial(_sample_select_kernel, n_tok=n_tok, n_past=n_past),
        grid=(n_seq,),
        in_specs=[pl.BlockSpec((1, n_tok, D_MODEL), lambda b: (b, 0, 0)),
                  pl.BlockSpec((n_past, D_MODEL), lambda b: (b, 0)),
                  _const_spec(hind.shape)],
        out_specs=pl.BlockSpec((1, n_tok, MOBA_TOPK, LANES), lambda b: (b, 0, 0, 0)),
        out_shape=jax.ShapeDtypeStruct((n_seq, n_tok, MOBA_TOPK, LANES), jnp.int32),
        compiler_params=_params(("arbitrary",)),
        name="sample_select",
    )(q3, pm, hind)


N_SEL_PAGES = MOBA_TOPK * PAGES_PER_BLOCK


def _sample_attn_kernel(sel_ref, pt_ref, *refs, n_tok, n_past):
    del pt_ref
    n_kv = 2 * N_SEL_PAGES
    k_refs = refs[:n_kv]
    v_refs = refs[n_kv:2 * n_kv]
    q_ref, kn_ref, vn_ref, near_ref, far_ref, ownb_ref, o_ref = refs[2 * n_kv:]
    b = pl.program_id(0)
    t = pl.program_id(1)
    hp = pl.program_id(2)
    qrow = q_ref[0, 0, 0]
    lane = lax.broadcasted_iota(jnp.int32, (1, LANES), 1)
    kn = kn_ref[0, 0]
    vn = vn_ref[0, 0]
    out = jnp.zeros((1, LANES), F32)
    for hs in range(2):
        h = 2 * hp + hs
        in_head = jnp.logical_and(lane >= HEAD_DIM * hs, lane < HEAD_DIM * (hs + 1))
        qm = jnp.where(in_head, qrow * (HEAD_DIM ** -0.5), 0.0)
        q8 = jnp.broadcast_to(qm, (SUBLANES, LANES)).astype(BF16)
        base = ((b * n_tok + t) * N_HEADS + h) * MOBA_TOPK
        logits = []
        for s in range(N_SEL_PAGES):
            lg = _dot_nt(q8, k_refs[hs * N_SEL_PAGES + s][0].astype(BF16))[0:1]
            blk = sel_ref[base + s // PAGES_PER_BLOCK]
            is_near = jnp.logical_and(blk == n_past - 1, (s % PAGES_PER_BLOCK) == PAGES_PER_BLOCK - 1)
            logits.append(lg + jnp.where(is_near, near_ref[t * N_HEADS + h], far_ref[h]))
        l_own = jnp.sum(kn * qm, axis=1, keepdims=True) + ownb_ref[t * N_HEADS + h]
        m = jnp.max(l_own, axis=0, keepdims=True)
        for lg in logits:
            m = jnp.maximum(m, jnp.max(lg, axis=1, keepdims=True))
        p_own = jnp.exp(l_own - m)
        den = jnp.sum(p_own, axis=0, keepdims=True)
        o = jnp.sum(p_own * vn, axis=0, keepdims=True)
        for s, lg in enumerate(logits):
            p = jnp.exp(lg - m)
            den = den + jnp.sum(p, axis=1, keepdims=True)
            p8 = jnp.broadcast_to(p, (SUBLANES, PAGE_SIZE)).astype(BF16)
            o = o + _dot(p8, v_refs[hs * N_SEL_PAGES + s][0].astype(BF16))[0:1]
        out = jnp.where(in_head, o / den, out)
    o_ref[0, 0, 0] = out


def _moba_sample(q, k_new, v_new, cache_k, cache_v, page_table, rel_bias):
    n_seq, n_pages = page_table.shape
    n_tok = q.shape[0] // n_seq
    past_len = n_pages * PAGE_SIZE
    assert past_len % MOBA_BLOCK == 0, "own block must start at the first new token"
    n_past = past_len // MOBA_BLOCK
    assert n_past >= MOBA_TOPK and (n_seq * n_pages) % PAGES_PER_STEP == 0
    n_pool = cache_k.shape[0]
    ck3 = cache_k.reshape(n_pool, PAGE_SIZE, D_MODEL)
    cv3 = cache_v.reshape(n_pool, PAGE_SIZE, D_MODEL)
    pt_flat = page_table.reshape(-1)
    pm = _block_means(ck3, pt_flat)
    q3 = q.reshape(n_seq, n_tok, D_MODEL)
    idx = _sample_select(q3, pm, n_past)
    sel = idx[..., :N_HEADS].transpose(0, 1, 3, 2).reshape(-1)
    n_hp = N_HEADS // 2
    assert PAGE_SIZE + 1 >= FAR_DIST
    bd = _bias_by_distance(rel_bias, PAGE_SIZE + n_tok)
    tt = jnp.arange(n_tok)
    near = bd[PAGE_SIZE + tt[:, None] - jnp.arange(PAGE_SIZE)[None, :]]
    near = near.transpose(0, 2, 1).reshape(n_tok * N_HEADS, 1, PAGE_SIZE)
    far = jnp.broadcast_to(rel_bias[REL_BUCKETS - 1][:, None, None], (N_HEADS, 1, PAGE_SIZE))
    dd = tt[:, None] - tt[None, :]
    ownb = jnp.where((dd >= 0)[..., None], bd[jnp.maximum(dd, 0)], NEG)
    ownb = ownb.transpose(0, 2, 1).reshape(n_tok * N_HEADS, n_tok, 1)
    q5 = q.reshape(n_seq, n_tok, n_hp, 1, LANES)
    kn4 = k_new.reshape(n_seq, n_tok, n_hp, LANES).transpose(0, 2, 1, 3)
    vn4 = v_new.reshape(n_seq, n_tok, n_hp, LANES).transpose(0, 2, 1, 3)

    def page_spec(slot):
        hs, i3, pg = slot // N_SEL_PAGES, (slot % N_SEL_PAGES) // PAGES_PER_BLOCK, slot % PAGES_PER_BLOCK

        def index_map(b, t, hp, sel_r, pt_r):
            blk = sel_r[((b * n_tok + t) * N_HEADS + 2 * hp + hs) * MOBA_TOPK + i3]
            return (pt_r[b * n_pages + blk * PAGES_PER_BLOCK + pg], 0, hp)

        return pl.BlockSpec((1, PAGE_SIZE, LANES), index_map)

    kv_specs = [page_spec(s) for s in range(2 * N_SEL_PAGES)]
    out = pl.pallas_call(
        functools.partial(_sample_attn_kernel, n_tok=n_tok, n_past=n_past),
        grid_spec=pltpu.PrefetchScalarGridSpec(
            num_scalar_prefetch=2, grid=(n_seq, n_tok, n_hp),
            in_specs=kv_specs + kv_specs + [
                pl.BlockSpec((1, 1, 1, 1, LANES), lambda b, t, hp, *_: (b, t, hp, 0, 0)),
                pl.BlockSpec((1, 1, n_tok, LANES), lambda b, t, hp, *_: (b, hp, 0, 0)),
                pl.BlockSpec((1, 1, n_tok, LANES), lambda b, t, hp, *_: (b, hp, 0, 0)),
                _const_spec(near.shape), _const_spec(far.shape), _const_spec(ownb.shape)],
            out_specs=pl.BlockSpec((1, 1, 1, 1, LANES), lambda b, t, hp, *_: (b, t, hp, 0, 0))),
        out_shape=jax.ShapeDtypeStruct((n_seq, n_tok, n_hp, 1, LANES), F32),
        compiler_params=_params(("arbitrary", "arbitrary", "arbitrary")),
        name="moba_sample",
    )(sel, pt_flat, *([ck3] * (2 * N_SEL_PAGES)), *([cv3] * (2 * N_SEL_PAGES)), q5, kn4, vn4, near, far, ownb)
    return out.reshape(n_seq * n_tok, D_MODEL)


def _moe_kernel(h3_ref, at_ref, p_ref, wo_ref, gf_ref, rt_ref, w1_ref, w3_ref, w2_ref, wple_ref, wgate_ref, gfin_ref,
                y_ref, h4_s, hn_s, gate_s, acc_s):
    e = pl.program_id(1)
    c = pl.program_id(2)

    @pl.when(jnp.logical_and(e == 0, c == 0))
    def _():
        h4 = h3_ref[...] + _dot(at_ref[...].astype(BF16), wo_ref[...])
        h4_s[...] = h4
        hn = _rmsn(h4) * gf_ref[...]
        hn_s[...] = hn.astype(BF16)
        hn_hi, hn_lo = _split2(hn)
        rt_hi, rt_lo = _split2(rt_ref[...])
        logits = _dot(hn_hi, rt_hi) + (_dot(hn_hi, rt_lo) + _dot(hn_lo, rt_hi))
        lane = lax.broadcasted_iota(jnp.int32, logits.shape, 1)
        logits = jnp.where(lane < N_EXPERTS, logits, -jnp.inf)
        ex = jnp.exp(logits - jnp.max(logits, axis=1, keepdims=True))
        probs = ex / jnp.sum(ex, axis=1, keepdims=True)
        p1 = jnp.max(probs, axis=1, keepdims=True)
        i1 = jnp.min(jnp.where(probs == p1, lane, LANES), axis=1, keepdims=True)
        rest = jnp.where(lane == i1, -1.0, probs)
        p2 = jnp.max(rest, axis=1, keepdims=True)
        i2 = jnp.min(jnp.where(rest == p2, lane, LANES), axis=1, keepdims=True)
        tot = p1 + p2
        gate_s[...] = jnp.where(lane == i1, p1 / tot, 0.0) + jnp.where(lane == i2, p2 / tot, 0.0)
        acc_s[...] = jnp.zeros_like(acc_s)

    hn = hn_s[...]
    a = _dot(hn, w1_ref[0])
    b = _dot(hn, w3_ref[0])
    y = _dot((jax.nn.silu(a) * b).astype(BF16), w2_ref[0])
    lane = lax.broadcasted_iota(jnp.int32, gate_s.shape, 1)
    g = jnp.sum(jnp.where(lane == e, gate_s[...], 0.0), axis=1, keepdims=True)
    acc_s[...] += g * y

    @pl.when(jnp.logical_and(e == N_EXPERTS - 1, c == 1))
    def _():
        h5 = h4_s[...] + acc_s[...]
        ple = _dot(p_ref[...].astype(BF16), wple_ref[...]) * jax.nn.sigmoid(_dot(h5.astype(BF16), wgate_ref[...]))
        y_ref[...] = _rmsn(h5 + ple) * gfin_ref[...]


def _moe_layer(h3, attn, p, wo, gf, rt, w1, w3, w2, wple, wgate, gfin, tm):
    n = h3.shape[0]
    tm = min(tm, n)
    tile = pl.BlockSpec((tm, D_MODEL), lambda i, e, c: (i, 0))
    return pl.pallas_call(
        _moe_kernel,
        grid=(n // tm, N_EXPERTS, 2),
        in_specs=[tile, tile, pl.BlockSpec((tm, PLE_DIM), lambda i, e, c: (i, 0)),
                  _const_spec(wo.shape), _const_spec(gf.shape), _const_spec(rt.shape),
                  pl.BlockSpec((1, D_MODEL, FF_HALF), lambda i, e, c: (e, 0, c)),
                  pl.BlockSpec((1, D_MODEL, FF_HALF), lambda i, e, c: (e, 0, c)),
                  pl.BlockSpec((1, FF_HALF, D_MODEL), lambda i, e, c: (e, c, 0)),
                  _const_spec(wple.shape), _const_spec(wgate.shape), _const_spec(gfin.shape)],
        out_specs=tile,
        out_shape=jax.ShapeDtypeStruct((n, D_MODEL), F32),
        scratch_shapes=[pltpu.VMEM((tm, D_MODEL), F32), pltpu.VMEM((tm, D_MODEL), BF16),
                        pltpu.VMEM((tm, LANES), F32), pltpu.VMEM((tm, D_MODEL), F32)],
        compiler_params=_params(("arbitrary", "arbitrary", "arbitrary")),
        name="oproj_moe",
    )(h3, attn, p, wo, gf, rt, w1, w3, w2, wple, wgate, gfin)


S5_CHUNK = 256
TOKEN_TILE = 512


def kernel(x_prompt, x_sample, cache_k, cache_v, state_ssm_re, state_ssm_im, page_table, p_prompt, p_sample,
           rel_bias, norm_mix, norm_ffn, norm_kv, norm_final, ssm_lam_re, ssm_lam_im, ssm_log_dt, ssm_b_re,
           ssm_b_im, ssm_c_re, ssm_c_im, ssm_d, w_glu_a, w_glu_b, w_q, w_k, w_v, w_o, ffn_w1, ffn_w3, ffn_w2,
           router, moe_w1, moe_w3, moe_w2, w_ple, w_ple_gate):
    assert x_prompt.shape[0] == 1 and norm_mix.shape[0] == 2, "one prompt sequence, depth 2"
    t_len = x_prompt.shape[1]
    n_seq, n_tok, _ = x_sample.shape
    row = lambda v: v.reshape(1, -1).astype(F32)
    bf = lambda w: w.astype(BF16)

    bblk, cre, cim, tabs, a1 = _s5_tables(ssm_lam_re[0], ssm_lam_im[0], ssm_log_dt[0], ssm_b_re[0], ssm_b_im[0],
                                          ssm_c_re[0], ssm_c_im[0])
    g_mix0, g_mix1 = row(norm_mix[0]), row(norm_mix[1])
    g_ffn0, g_ffn1 = row(norm_ffn[0]), row(norm_ffn[1])
    g_kv, g_fin, d_skip = row(norm_kv), row(norm_final), row(ssm_d[0])
    wa, wb = bf(w_glu_a[0]), bf(w_glu_b[0])
    f_w1, f_w3, f_w2 = bf(ffn_w1[0]), bf(ffn_w3[0]), bf(ffn_w2[0])
    wple0, wple1 = bf(w_ple[0]), bf(w_ple[1])
    wgate0, wgate1 = bf(w_ple_gate[0]), bf(w_ple_gate[1])
    wk, wv, wq, wo = bf(w_k), bf(w_v), bf(w_q[0]), bf(w_o[0])
    m_w1, m_w3, m_w2 = bf(moe_w1[0]), bf(moe_w3[0]), bf(moe_w2[0])
    rt = jnp.pad(router[0].astype(F32), ((0, 0), (0, LANES - N_EXPERTS)))

    def ffn0(h1, p, with_kmean):
        return _ffn0(h1, p, g_ffn0, f_w1, f_w3, f_w2, wple0, wgate0, g_kv, g_mix1, wk, wv, wq,
                     TOKEN_TILE, with_kmean)

    def moe(h3, attn, p):
        return _moe_layer(h3, attn, p, wo, g_ffn1, rt, m_w1, m_w3, m_w2, wple1, wgate1, g_fin, TOKEN_TILE)

    h1, sre_p, sim_p = _s5_prompt(x_prompt[0].astype(F32), g_mix0, bblk, cre, cim, tabs, d_skip, wa, wb, S5_CHUNK)
    h3, k_p, v_p, q_p, kmean = ffn0(h1, p_prompt[0, 0], True)
    attn_p = _moba_prompt(q_p, k_p, v_p, kmean.reshape(-1, D_MODEL), rel_bias)
    y_p = moe(h3, attn_p, p_prompt[1, 0])

    n_s = n_seq * n_tok
    x_tm = x_sample.astype(F32).transpose(1, 0, 2).reshape(n_s, D_MODEL)
    h1s, sre_s, sim_s = _s5_sample(x_tm, g_mix0, bblk, cre, cim, a1,
                                   state_ssm_re[0].reshape(n_seq, N_STATE), state_ssm_im[0].reshape(n_seq, N_STATE),
                                   d_skip, wa, wb, n_seq, n_tok)
    h1s = h1s.reshape(n_tok, n_seq, D_MODEL).transpose(1, 0, 2).reshape(n_s, D_MODEL)
    h3s, k_s, v_s, q_s = ffn0(h1s, p_sample[0].reshape(n_s, PLE_DIM), False)
    attn_s = _moba_sample(q_s, k_s, v_s, cache_k, cache_v, page_table, rel_bias)
    y_s = moe(h3s, attn_s, p_sample[1].reshape(n_s, PLE_DIM))

    st = lambda s, nb: s.reshape(1, nb, N_GROUPS, SSM_STATE)
    hd = lambda a, *lead: a.reshape(*lead, N_HEADS, HEAD_DIM)
    return (y_p.reshape(1, t_len, D_MODEL), y_s.reshape(n_seq, n_tok, D_MODEL),
            hd(k_p, 1, t_len), hd(v_p, 1, t_len), st(sre_p, 1), st(sim_p, 1),
            hd(k_s, n_seq, n_tok), hd(v_s, n_seq, n_tok), st(sre_s, n_seq), st(sim_s, n_seq))
```

```python
import functools
import math

import jax
import jax.numpy as jnp
import numpy as np
from jax import lax
from jax.experimental import pallas as pl
from jax.experimental.pallas import tpu as pltpu

F32 = jnp.float32
BF16 = jnp.bfloat16

D_MODEL = 1024
SSM_GROUP = 16
N_GROUPS = D_MODEL // SSM_GROUP
SSM_STATE = 64
N_STATE = N_GROUPS * SSM_STATE
S5_BLOCKS = 4
S5_IN = D_MODEL // S5_BLOCKS
S5_ST = N_STATE // S5_BLOCKS
N_HEADS = 16
HEAD_DIM = 64
MOBA_BLOCK = 256
MOBA_TOPK = 3
PAGE_SIZE = 128
PAGES_PER_BLOCK = MOBA_BLOCK // PAGE_SIZE
REL_BUCKETS = 32
REL_MAX_DIST = 128
D_FF = 2816
FF_HALF = D_FF // 2
N_EXPERTS = 8
PLE_DIM = 256
EPS = 1e-6
NEG = -1e30
LOG2E = math.log2(math.e)
SUBLANES = 8
LANES = 128
VMEM_LIMIT = 56 * 1024 * 1024
PAGES_PER_STEP = 8
FAR_GROUP = 4
DEN_ROWS = 16


def _rmsn(x):
    return x * lax.rsqrt(jnp.mean(x * x, axis=-1, keepdims=True) + EPS)


def _dot(a, b):
    return jnp.dot(a, b, preferred_element_type=F32)


def _dot_nt(a, b):
    return lax.dot_general(a, b, (((1,), (1,)), ((), ())), preferred_element_type=F32)


def _split2(x):
    hi = x.astype(BF16)
    lo = (x - hi.astype(F32)).astype(BF16)
    return hi, lo


def _const_spec(shape):
    nd = len(shape)
    return pl.BlockSpec(shape, lambda *_: (0,) * nd, pipeline_mode=pl.Buffered(1))


def _params(sem, vmem=VMEM_LIMIT):
    return pltpu.CompilerParams(dimension_semantics=sem, vmem_limit_bytes=vmem)


def _s5_tables(lam_re, lam_im, log_dt, b_re, b_im, c_re, c_im):
    dt = jnp.exp(log_dt.astype(F32))[:, None]
    lr = lam_re.astype(F32)
    li = lam_im.astype(F32)
    mag = jnp.exp(lr * dt)
    a_re = mag * jnp.cos(li * dt)
    a_im = mag * jnp.sin(li * dt)
    den = lr * lr + li * li
    n_re = a_re - 1.0
    k_re = (n_re * lr + a_im * li) / den
    k_im = (a_im * lr - n_re * li) / den
    bb_re = k_re[..., None] * b_re - k_im[..., None] * b_im
    bb_im = k_re[..., None] * b_im + k_im[..., None] * b_re
    gpb = N_GROUPS // S5_BLOCKS
    eye = jnp.eye(gpb, dtype=F32)

    def pack_b(bb):
        t = bb.reshape(S5_BLOCKS, gpb, SSM_STATE, SSM_GROUP)
        t = jnp.einsum('jgpc,gh->jgchp', t, eye)
        return t.reshape(S5_BLOCKS, S5_IN, S5_ST)

    def pack_c(cc):
        t = cc.astype(F32).reshape(S5_BLOCKS, gpb, SSM_GROUP, SSM_STATE)
        t = jnp.einsum('jgcp,gh->jgphc', t, eye)
        return t.reshape(S5_BLOCKS, S5_ST, S5_IN)

    bblk = jnp.concatenate([pack_b(bb_re), pack_b(bb_im)], axis=-1).astype(BF16)
    cre = pack_c(c_re).astype(BF16)
    cim = (-pack_c(c_im)).astype(BF16)
    ar = a_re.reshape(1, N_STATE)
    ai = a_im.reshape(1, N_STATE)
    pw = [(ar, ai)]
    for _ in range(SUBLANES - 1):
        pr, pi = pw[-1]
        pw.append((pr * ar - pi * ai, pr * ai + pi * ar))
    row = jnp.arange(SUBLANES)[:, None]

    def masked(k):
        return (jnp.where(row >= k, pw[k - 1][0], 0.0), jnp.where(row >= k, pw[k - 1][1], 0.0))

    m1, m2, m4 = masked(1), masked(2), masked(4)
    a_r = jnp.concatenate([p[0] for p in pw], axis=0)
    a_i = jnp.concatenate([p[1] for p in pw], axis=0)
    tabs = jnp.stack([m1[0], m1[1], m2[0], m2[1], m4[0], m4[1], a_r, a_i])
    a1 = jnp.concatenate([ar, ai], axis=0)
    return bblk, cre, cim, tabs, a1


def _s5_out(x, u, y, d_ref, wa_ref, wb_ref):
    y = y + d_ref[...] * u
    hb = jax.nn.gelu(y).astype(BF16)
    return x + _dot(hb, wa_ref[...]) * jax.nn.sigmoid(_dot(hb, wb_ref[...]))


def _s5_prompt_kernel(x_ref, g_ref, bblk_ref, cre_ref, cim_ref, tabs_ref, d_ref, wa_ref, wb_ref,
                      h_ref, sre_ref, sim_ref, cr_s, ci_s, xr_s, xi_s):
    @pl.when(pl.program_id(0) == 0)
    def _():
        cr_s[...] = jnp.zeros_like(cr_s)
        ci_s[...] = jnp.zeros_like(ci_s)

    x = x_ref[...]
    u = _rmsn(x) * g_ref[...]
    ub = u.astype(BF16)
    n8 = x.shape[0] // SUBLANES
    ys = []
    for j in range(S5_BLOCKS):
        lo, hi = j * S5_ST, (j + 1) * S5_ST
        bu = _dot(ub[:, j * S5_IN:(j + 1) * S5_IN], bblk_ref[j])
        xr = bu[:, :S5_ST].reshape(n8, SUBLANES, S5_ST)
        xi = bu[:, S5_ST:].reshape(n8, SUBLANES, S5_ST)
        for k, sh in ((0, 1), (2, 2), (4, 4)):
            mr = tabs_ref[k, :, lo:hi]
            mi = tabs_ref[k + 1, :, lo:hi]
            rr = pltpu.roll(xr, sh, 1)
            ri = pltpu.roll(xi, sh, 1)
            xr, xi = xr + (mr * rr - mi * ri), xi + (mr * ri + mi * rr)
        a_r = tabs_ref[6, :, lo:hi]
        a_i = tabs_ref[7, :, lo:hi]
        pr = cr_s[:, lo:hi]
        pi = ci_s[:, lo:hi]
        for t in range(n8):
            br = jnp.broadcast_to(pr, (SUBLANES, S5_ST))
            bi = jnp.broadcast_to(pi, (SUBLANES, S5_ST))
            tr = xr[t] + (a_r * br - a_i * bi)
            ti = xi[t] + (a_r * bi + a_i * br)
            xr_s[t * SUBLANES:(t + 1) * SUBLANES, :] = tr
            xi_s[t * SUBLANES:(t + 1) * SUBLANES, :] = ti
            pr = tr[SUBLANES - 1:SUBLANES, :]
            pi = ti[SUBLANES - 1:SUBLANES, :]
        cr_s[:, lo:hi] = pr
        ci_s[:, lo:hi] = pi
        ys.append(_dot(xr_s[...].astype(BF16), cre_ref[j]) + _dot(xi_s[...].astype(BF16), cim_ref[j]))
    y = jnp.concatenate(ys, axis=1)
    h_ref[...] = _s5_out(x, u, y, d_ref, wa_ref, wb_ref)
    sre_ref[...] = cr_s[...]
    sim_ref[...] = ci_s[...]


def _s5_prompt(x, g, bblk, cre, cim, tabs, d, wa, wb, chunk):
    t_len = x.shape[0]
    chunk = min(chunk, t_len)
    return pl.pallas_call(
        _s5_prompt_kernel,
        grid=(t_len // chunk,),
        in_specs=[pl.BlockSpec((chunk, D_MODEL), lambda i: (i, 0)),
                  _const_spec(g.shape), _const_spec(bblk.shape), _const_spec(cre.shape), _const_spec(cim.shape),
                  _const_spec(tabs.shape), _const_spec(d.shape), _const_spec(wa.shape), _const_spec(wb.shape)],
        out_specs=[pl.BlockSpec((chunk, D_MODEL), lambda i: (i, 0)),
                   pl.BlockSpec((1, N_STATE), lambda i: (0, 0)),
                   pl.BlockSpec((1, N_STATE), lambda i: (0, 0))],
        out_shape=[jax.ShapeDtypeStruct((t_len, D_MODEL), F32),
                   jax.ShapeDtypeStruct((1, N_STATE), F32),
                   jax.ShapeDtypeStruct((1, N_STATE), F32)],
        scratch_shapes=[pltpu.VMEM((1, N_STATE), F32), pltpu.VMEM((1, N_STATE), F32),
                        pltpu.VMEM((chunk, S5_ST), F32), pltpu.VMEM((chunk, S5_ST), F32)],
        compiler_params=_params(("arbitrary",)),
        name="s5_prompt",
    )(x, g, bblk, cre, cim, tabs, d, wa, wb)


def _s5_sample_kernel(x_ref, g_ref, bblk_ref, cre_ref, cim_ref, a_ref, s0r_ref, s0i_ref, d_ref, wa_ref, wb_ref,
                      h_ref, sre_ref, sim_ref, xr_s, xi_s, *, n_seq, n_tok):
    x = x_ref[...]
    u = _rmsn(x) * g_ref[...]
    ub = u.astype(BF16)
    ys = []
    for j in range(S5_BLOCKS):
        lo, hi = j * S5_ST, (j + 1) * S5_ST
        bu = _dot(ub[:, j * S5_IN:(j + 1) * S5_IN], bblk_ref[j])
        ar = a_ref[0:1, lo:hi]
        ai = a_ref[1:2, lo:hi]
        sr = s0r_ref[:, lo:hi]
        si = s0i_ref[:, lo:hi]
        for t in range(n_tok):
            rows = bu[t * n_seq:(t + 1) * n_seq]
            sr, si = ar * sr - ai * si + rows[:, :S5_ST], ar * si + ai * sr + rows[:, S5_ST:]
            xr_s[t * n_seq:(t + 1) * n_seq, :] = sr
            xi_s[t * n_seq:(t + 1) * n_seq, :] = si
        sre_ref[:, lo:hi] = sr
        sim_ref[:, lo:hi] = si
        ys.append(_dot(xr_s[...].astype(BF16), cre_ref[j]) + _dot(xi_s[...].astype(BF16), cim_ref[j]))
    y = jnp.concatenate(ys, axis=1)
    h_ref[...] = _s5_out(x, u, y, d_ref, wa_ref, wb_ref)


def _s5_sample(x_tm, g, bblk, cre, cim, a1, s0r, s0i, d, wa, wb, n_seq, n_tok):
    n = n_seq * n_tok
    args = (x_tm, g, bblk, cre, cim, a1, s0r, s0i, d, wa, wb)
    return pl.pallas_call(
        functools.partial(_s5_sample_kernel, n_seq=n_seq, n_tok=n_tok),
        grid=(1,),
        in_specs=[_const_spec(a.shape) for a in args],
        out_specs=[_const_spec((n, D_MODEL)), _const_spec((n_seq, N_STATE)), _const_spec((n_seq, N_STATE))],
        out_shape=[jax.ShapeDtypeStruct((n, D_MODEL), F32),
                   jax.ShapeDtypeStruct((n_seq, N_STATE), F32),
                   jax.ShapeDtypeStruct((n_seq, N_STATE), F32)],
        scratch_shapes=[pltpu.VMEM((n, S5_ST), F32), pltpu.VMEM((n, S5_ST), F32)],
        compiler_params=_params(("arbitrary",)),
        name="s5_sample",
    )(*args)


def _ffn0_kernel(h1_ref, p_ref, gf_ref, w1_ref, w3_ref, w2_ref, wple_ref, wgate_ref, gkv_ref, gq_ref,
                 wk_ref, wv_ref, wq_ref, h3_ref, k_ref, v_ref, q_ref, *km_ref, n_blk):
    x = h1_ref[...]
    hn = (_rmsn(x) * gf_ref[...]).astype(BF16)
    h2 = x
    for c in range(2):
        lo, hi = c * FF_HALF, (c + 1) * FF_HALF
        a = _dot(hn, w1_ref[:, lo:hi])
        b = _dot(hn, w3_ref[:, lo:hi])
        h2 = h2 + _dot((jax.nn.silu(a) * b).astype(BF16), w2_ref[lo:hi, :])
    ple = _dot(p_ref[...].astype(BF16), wple_ref[...]) * jax.nn.sigmoid(_dot(h2.astype(BF16), wgate_ref[...]))
    h3 = h2 + ple
    h3_ref[...] = h3
    xn = _rmsn(h3)
    kvb = (xn * gkv_ref[...]).astype(BF16)
    qb = (xn * gq_ref[...]).astype(BF16)
    k = _dot(kvb, wk_ref[...])
    k_ref[...] = k
    v_ref[...] = _dot(kvb, wv_ref[...])
    q_ref[...] = _dot(qb, wq_ref[...])
    for b in range(n_blk):
        km_ref[0][b] = jnp.mean(k[b * MOBA_BLOCK:(b + 1) * MOBA_BLOCK], axis=0, keepdims=True)


def _ffn0(h1, p, gf, w1, w3, w2, wple, wgate, gkv, gq, wk, wv, wq, tm, with_kmean):
    n = h1.shape[0]
    tm = min(tm, n)
    n_blk = tm // MOBA_BLOCK if with_kmean else 0
    consts = (gf, w1, w3, w2, wple, wgate, gkv, gq, wk, wv, wq)
    tile = pl.BlockSpec((tm, D_MODEL), lambda i: (i, 0))
    out_specs = [tile, tile, tile, tile]
    out_shape = [jax.ShapeDtypeStruct((n, D_MODEL), F32)] * 4
    if with_kmean:
        out_specs.append(pl.BlockSpec((n_blk, 1, D_MODEL), lambda i: (i, 0, 0)))
        out_shape.append(jax.ShapeDtypeStruct((n // MOBA_BLOCK, 1, D_MODEL), F32))
    return pl.pallas_call(
        functools.partial(_ffn0_kernel, n_blk=n_blk),
        grid=(n // tm,),
        in_specs=[tile, pl.BlockSpec((tm, PLE_DIM), lambda i: (i, 0))] + [_const_spec(c.shape) for c in consts],
        out_specs=out_specs,
        out_shape=out_shape,
        compiler_params=_params(("arbitrary",)),
        name="ffn0_qkv",
    )(h1, p, *consts)


def _rel_bucket(dist):
    max_exact = REL_BUCKETS // 2
    d = jnp.maximum(dist, 0)
    large = max_exact + (jnp.log(jnp.maximum(d, max_exact).astype(F32) / max_exact)
                         / math.log(REL_MAX_DIST / max_exact) * (REL_BUCKETS - max_exact)).astype(jnp.int32)
    large = jnp.minimum(large, REL_BUCKETS - 1)
    return jnp.where(d < max_exact, d, large)


def _bias_by_distance(rel_bias, n):
    return rel_bias[_rel_bucket(jnp.arange(n))]


FAR_DIST = REL_MAX_DIST


def _top3_rows(cur, rows, n_rows):
    picked = jnp.zeros(cur.shape, jnp.bool_)
    for _ in range(MOBA_TOPK):
        m = jnp.max(cur, axis=0, keepdims=True)
        idx = jnp.min(jnp.where(cur == m, rows, n_rows), axis=0, keepdims=True)
        hit = rows == idx
        picked = jnp.logical_or(picked, hit)
        cur = jnp.where(hit, -jnp.inf, cur)
    return picked


def _moba_prompt_kernel(q_ref, k_ref, v_ref, km_ref, bias_ref, o_ref, sel_s, s_s, smax_s, *, n_blk):
    own = pl.program_id(1)
    q = q_ref[0]
    tq = q.shape[0]
    q_hi, q_lo = _split2(q)
    km_hi, km_lo = _split2(km_ref[0])
    sc = _dot_nt(km_hi, q_hi) + (_dot_nt(km_hi, q_lo) + _dot_nt(km_lo, q_hi))
    rows = lax.broadcasted_iota(jnp.int32, sc.shape, 0)
    past = rows < own
    picked = _top3_rows(jnp.where(past, sc, NEG), rows, n_blk)
    sel = jnp.where(jnp.logical_and(picked, past), 1.0, 0.0)
    for n in range(n_blk):
        sel_s[n] = sel[n:n + 1, :]

    qb = (q * (HEAD_DIM ** -0.5 * LOG2E)).astype(BF16)
    grp = FAR_GROUP

    def softmax_step(carry, s3, smax, on3, vs):
        m_run, acc = carry
        m_new = jnp.maximum(m_run, jnp.max(jnp.where(on3, smax, NEG), axis=0))
        alpha = jnp.exp2(m_run - m_new)
        pb = jnp.exp2(s3 - jnp.where(on3, m_new[None], -NEG)).astype(BF16)
        acc = alpha * acc
        for j, vj in enumerate(vs):
            acc = acc + _dot(vj, pb[j])
        return m_new, acc

    n_far = jnp.maximum(own - 1, 0)
    blk_iota = lax.broadcasted_iota(jnp.int32, (grp, 1, tq), 0)

    def produce(g, slot):
        start = jnp.minimum(g * grp, n_blk - grp)
        for j in range(grp):
            s = _dot_nt(k_ref[0, start + j], qb)
            s_s[slot, j] = s
            smax_s[slot, j] = jnp.max(s, axis=0, keepdims=True)

    def consume(g, slot, carry):
        start = jnp.minimum(g * grp, n_blk - grp)
        on3 = jnp.logical_and(sel_s[pl.ds(start, grp)] > 0.0, blk_iota + g * grp < n_far)
        return softmax_step(carry, s_s[slot], smax_s[slot], on3, [v_ref[0, start + j] for j in range(grp)])

    def far_body(i, carry):
        produce(2 * i + 1, 1)
        carry = consume(2 * i, 0, carry)
        produce(2 * i + 2, 0)
        return consume(2 * i + 1, 1, carry)

    init = (jnp.full((1, tq), NEG, F32), jnp.zeros((HEAD_DIM + DEN_ROWS, tq), F32))
    produce(0, 0)
    n_grp = (n_far + grp - 1) // grp
    carry = lax.fori_loop(0, (n_grp + 1) // 2, far_body, init)
    prev = jnp.maximum(own - 1, 0)
    k2 = jnp.concatenate([k_ref[0, prev], k_ref[0, own]], axis=0)
    s2 = _dot_nt(k2, qb).reshape(2, MOBA_BLOCK, tq) + bias_ref[0]
    on2 = jnp.concatenate([sel_s[pl.ds(prev, 1)] > 0.0, jnp.ones((1, 1, tq), jnp.bool_)], axis=0)
    _, acc = softmax_step(carry, s2, jnp.max(s2, axis=1, keepdims=True), on2, [v_ref[0, prev], v_ref[0, own]])
    o_ref[0] = acc[:HEAD_DIM] / acc[HEAD_DIM:HEAD_DIM + 1]


def _moba_prompt(q, k, v, kmean, rel_bias):
    t_len = q.shape[0]
    n_blk = t_len // MOBA_BLOCK
    qh = q.reshape(t_len, N_HEADS, HEAD_DIM).transpose(1, 0, 2)
    kh = k.astype(BF16).reshape(n_blk, MOBA_BLOCK, N_HEADS, HEAD_DIM).transpose(2, 0, 1, 3)
    vt = v.astype(BF16).reshape(n_blk, MOBA_BLOCK, N_HEADS, HEAD_DIM).transpose(2, 0, 3, 1)
    vt = jnp.concatenate([vt, jnp.ones((N_HEADS, n_blk, DEN_ROWS, MOBA_BLOCK), BF16)], axis=2)
    kmh = kmean.reshape(n_blk, N_HEADS, HEAD_DIM).transpose(1, 0, 2)
    bd = _bias_by_distance(rel_bias, 2 * MOBA_BLOCK)
    far = rel_bias[REL_BUCKETS - 1]
    dmat = jnp.arange(MOBA_BLOCK)[None, :] - jnp.arange(MOBA_BLOCK)[:, None]
    own_t = jnp.where((dmat >= 0)[..., None], (bd[jnp.maximum(dmat, 0)] - far) * LOG2E, NEG)
    prev_t = (bd[dmat + MOBA_BLOCK] - far) * LOG2E
    bias = jnp.stack([prev_t, own_t]).transpose(3, 0, 1, 2)
    assert n_blk % FAR_GROUP == 0
    out = pl.pallas_call(
        functools.partial(_moba_prompt_kernel, n_blk=n_blk),
        grid=(N_HEADS, n_blk),
        in_specs=[pl.BlockSpec((1, MOBA_BLOCK, HEAD_DIM), lambda h, i: (h, i, 0)),
                  pl.BlockSpec((1, n_blk, MOBA_BLOCK, HEAD_DIM), lambda h, i: (h, 0, 0, 0)),
                  pl.BlockSpec((1, n_blk, HEAD_DIM + DEN_ROWS, MOBA_BLOCK), lambda h, i: (h, 0, 0, 0)),
                  pl.BlockSpec((1, n_blk, HEAD_DIM), lambda h, i: (h, 0, 0)),
                  pl.BlockSpec((1, 2, MOBA_BLOCK, MOBA_BLOCK), lambda h, i: (h, 0, 0, 0))],
        out_specs=pl.BlockSpec((1, HEAD_DIM, MOBA_BLOCK), lambda h, i: (h, 0, i)),
        out_shape=jax.ShapeDtypeStruct((N_HEADS, HEAD_DIM, t_len), F32),
        scratch_shapes=[pltpu.VMEM((n_blk, 1, MOBA_BLOCK), F32),
                        pltpu.VMEM((2, FAR_GROUP, MOBA_BLOCK, MOBA_BLOCK), F32),
                        pltpu.VMEM((2, FAR_GROUP, 1, MOBA_BLOCK), F32)],
        compiler_params=_params(("arbitrary", "arbitrary")),
        name="moba_prompt",
    )(qh, kh, vt, kmh, bias)
    return out.transpose(2, 0, 1).reshape(t_len, D_MODEL)


def _page_mean_kernel(pt_ref, *refs):
    del pt_ref
    o_ref = refs[-1]
    for b in range(PAGES_PER_STEP // PAGES_PER_BLOCK):
        s = jnp.sum(refs[2 * b][0], axis=0) + jnp.sum(refs[2 * b + 1][0], axis=0)
        o_ref[b] = s * (1.0 / MOBA_BLOCK)


def _block_means(cache_k, pt_flat):
    n_pages_total = pt_flat.shape[0]
    steps = n_pages_total // PAGES_PER_STEP
    blocks_per_step = PAGES_PER_STEP // PAGES_PER_BLOCK

    def page_spec(r):
        return pl.BlockSpec((1, PAGE_SIZE, N_HEADS, HEAD_DIM),
                            lambda i, pt: (pt[i * PAGES_PER_STEP + r], 0, 0, 0))

    return pl.pallas_call(
        _page_mean_kernel,
        grid_spec=pltpu.PrefetchScalarGridSpec(
            num_scalar_prefetch=1, grid=(steps,),
            in_specs=[page_spec(r) for r in range(PAGES_PER_STEP)],
            out_specs=pl.BlockSpec((blocks_per_step, N_HEADS, HEAD_DIM), lambda i, pt: (i, 0, 0))),
        out_shape=jax.ShapeDtypeStruct((n_pages_total // PAGES_PER_BLOCK, N_HEADS, HEAD_DIM), F32),
        compiler_params=_params(("arbitrary",)),
        name="page_means",
    )(pt_flat, *([cache_k] * PAGES_PER_STEP))


def _sample_select_kernel(q_ref, pm_ref, ones_ref, idx_ref, *, n_tok, n_past):
    pm = pm_ref[...]
    blk = lax.broadcasted_iota(jnp.int32, (n_past, N_HEADS, LANES), 0)
    for t in range(n_tok):
        prod = (pm * q_ref[0, t]).reshape(n_past * N_HEADS, HEAD_DIM)
        p1 = prod.astype(BF16)
        r1 = prod - p1.astype(F32)
        p2 = r1.astype(BF16)
        p3 = (r1 - p2.astype(F32)).astype(BF16)
        sc = _dot(p1, ones_ref[...]) + (_dot(p2, ones_ref[...]) + _dot(p3, ones_ref[...]))
        cur = sc.reshape(n_past, N_HEADS, LANES)
        for i in range(MOBA_TOPK):
            m = jnp.max(cur, axis=0, keepdims=True)
            idx = jnp.min(jnp.where(cur == m, blk, n_past), axis=0)
            idx_ref[0, t, i] = idx
            cur = jnp.where(blk == idx[None], -jnp.inf, cur)


def _sample_select(q4, pm, n_past):
    n_seq, n_tok = q4.shape[:2]
    ones = jnp.ones((HEAD_DIM, LANES), BF16)
    return pl.pallas_call(
        functools.partial(_sample_select_kernel, n_tok=n_tok, n_past=n_past),
        grid=(n_seq,),
        in_specs=[pl.BlockSpec((1, n_tok, N_HEADS, HEAD_DIM), lambda b: (b, 0, 0, 0)),
                  pl.BlockSpec((n_past, N_HEADS, HEAD_DIM), lambda b: (b, 0, 0)),
                  _const_spec(ones.shape)],
        out_specs=pl.BlockSpec((1, n_tok, MOBA_TOPK, N_HEADS, LANES), lambda b: (b, 0, 0, 0, 0)),
        out_shape=jax.ShapeDtypeStruct((n_seq, n_tok, MOBA_TOPK, N_HEADS, LANES), jnp.int32),
        compiler_params=_params(("arbitrary",)),
        name="sample_select",
    )(q4, pm, ones)


N_SEL_PAGES = MOBA_TOPK * PAGES_PER_BLOCK


def _sample_attn_kernel(sel_ref, pt_ref, ck_ref, cv_ref, q_ref, kn_ref, vn_ref, near_ref, far_ref, ownb_ref,
                        o_ref, kbuf, vbuf, sems, *, n_tok, n_past, n_pages):
    b = pl.program_id(0)
    t = pl.program_id(1)

    def head_copies(h, slot):
        base = ((b * n_tok + t) * N_HEADS + h) * MOBA_TOPK
        cps = []
        for s in range(N_SEL_PAGES):
            blk = sel_ref[base + s // PAGES_PER_BLOCK]
            page = pt_ref[b * n_pages + blk * PAGES_PER_BLOCK + s % PAGES_PER_BLOCK]
            cps.append(pltpu.make_async_copy(ck_ref.at[page, :, h, :], kbuf.at[slot, s], sems.at[0, slot, s]))
            cps.append(pltpu.make_async_copy(cv_ref.at[page, :, h, :], vbuf.at[slot, s], sems.at[1, slot, s]))
        return cps

    for cp in head_copies(0, 0):
        cp.start()
    for h in range(N_HEADS):
        slot = h % 2
        if h + 1 < N_HEADS:
            for cp in head_copies(h + 1, 1 - slot):
                cp.start()
        for cp in head_copies(h, slot):
            cp.wait()
        qs = q_ref[0, 0, h:h + 1, :] * (HEAD_DIM ** -0.5)
        q8 = jnp.broadcast_to(qs, (SUBLANES, HEAD_DIM)).astype(BF16)
        base = ((b * n_tok + t) * N_HEADS + h) * MOBA_TOPK
        logits = []
        for s in range(N_SEL_PAGES):
            lg = _dot_nt(q8, kbuf[slot, s].astype(BF16))[0:1]
            blk = sel_ref[base + s // PAGES_PER_BLOCK]
            is_near = jnp.logical_and(blk == n_past - 1, (s % PAGES_PER_BLOCK) == PAGES_PER_BLOCK - 1)
            logits.append(lg + jnp.where(is_near, near_ref[t * N_HEADS + h], far_ref[h]))
        kn = kn_ref[0, :, h, :]
        vn = vn_ref[0, :, h, :]
        l_own = jnp.sum(kn * qs, axis=1, keepdims=True) + ownb_ref[t * N_HEADS + h]
        m = jnp.max(l_own, axis=0, keepdims=True)
        for lg in logits:
            m = jnp.maximum(m, jnp.max(lg, axis=1, keepdims=True))
        p_own = jnp.exp(l_own - m)
        den = jnp.sum(p_own, axis=0, keepdims=True)
        o = jnp.sum(p_own * vn, axis=0, keepdims=True)
        for s, lg in enumerate(logits):
            p = jnp.exp(lg - m)
            den = den + jnp.sum(p, axis=1, keepdims=True)
            p8 = jnp.broadcast_to(p, (SUBLANES, PAGE_SIZE)).astype(BF16)
            o = o + _dot(p8, vbuf[slot, s].astype(BF16))[0:1]
        o_ref[0, 0, h:h + 1, :] = o / den


def _moba_sample(q, k_new, v_new, cache_k, cache_v, page_table, rel_bias):
    n_seq, n_pages = page_table.shape
    n_tok = q.shape[0] // n_seq
    past_len = n_pages * PAGE_SIZE
    assert past_len % MOBA_BLOCK == 0, "own block must start at the first new token"
    n_past = past_len // MOBA_BLOCK
    assert n_past >= MOBA_TOPK and (n_seq * n_pages) % PAGES_PER_STEP == 0
    pt_flat = page_table.reshape(-1)
    pm = _block_means(cache_k, pt_flat)
    q4 = q.reshape(n_seq, n_tok, N_HEADS, HEAD_DIM)
    idx = _sample_select(q4, pm, n_past)
    sel = idx[..., 0].transpose(0, 1, 3, 2).reshape(-1)
    assert PAGE_SIZE + 1 >= FAR_DIST
    bd = _bias_by_distance(rel_bias, PAGE_SIZE + n_tok)
    tt = jnp.arange(n_tok)
    near = bd[PAGE_SIZE + tt[:, None] - jnp.arange(PAGE_SIZE)[None, :]]
    near = near.transpose(0, 2, 1).reshape(n_tok * N_HEADS, 1, PAGE_SIZE)
    far = jnp.broadcast_to(rel_bias[REL_BUCKETS - 1][:, None, None], (N_HEADS, 1, PAGE_SIZE))
    dd = tt[:, None] - tt[None, :]
    ownb = jnp.where((dd >= 0)[..., None], bd[jnp.maximum(dd, 0)], NEG)
    ownb = ownb.transpose(0, 2, 1).reshape(n_tok * N_HEADS, n_tok, 1)
    kn4 = k_new.reshape(n_seq, n_tok, N_HEADS, HEAD_DIM)
    vn4 = v_new.reshape(n_seq, n_tok, N_HEADS, HEAD_DIM)
    tok_spec = pl.BlockSpec((1, 1, N_HEADS, HEAD_DIM), lambda b, t, *_: (b, t, 0, 0))
    new_spec = pl.BlockSpec((1, n_tok, N_HEADS, HEAD_DIM), lambda b, t, *_: (b, 0, 0, 0))
    out = pl.pallas_call(
        functools.partial(_sample_attn_kernel, n_tok=n_tok, n_past=n_past, n_pages=n_pages),
        grid_spec=pltpu.PrefetchScalarGridSpec(
            num_scalar_prefetch=2, grid=(n_seq, n_tok),
            in_specs=[pl.BlockSpec(memory_space=pl.ANY), pl.BlockSpec(memory_space=pl.ANY),
                      tok_spec, new_spec, new_spec,
                      _const_spec(near.shape), _const_spec(far.shape), _const_spec(ownb.shape)],
            out_specs=tok_spec,
            scratch_shapes=[pltpu.VMEM((2, N_SEL_PAGES, PAGE_SIZE, HEAD_DIM), F32),
                            pltpu.VMEM((2, N_SEL_PAGES, PAGE_SIZE, HEAD_DIM), F32),
                            pltpu.SemaphoreType.DMA((2, 2, N_SEL_PAGES))]),
        out_shape=jax.ShapeDtypeStruct((n_seq, n_tok, N_HEADS, HEAD_DIM), F32),
        compiler_params=_params(("arbitrary", "arbitrary")),
        name="moba_sample",
    )(sel, pt_flat, cache_k, cache_v, q4, kn4, vn4, near, far, ownb)
    return out.reshape(n_seq * n_tok, D_MODEL)


def _moe_kernel(h3_ref, at_ref, p_ref, wo_ref, gf_ref, rt_ref, w1_ref, w3_ref, w2_ref, wple_ref, wgate_ref, gfin_ref,
                y_ref, h4_s, hn_s, gate_s, acc_s):
    e = pl.program_id(1)
    c = pl.program_id(2)

    @pl.when(jnp.logical_and(e == 0, c == 0))
    def _():
        h4 = h3_ref[...] + _dot(at_ref[...].astype(BF16), wo_ref[...])
        h4_s[...] = h4
        hn = _rmsn(h4) * gf_ref[...]
        hn_s[...] = hn.astype(BF16)
        hn_hi, hn_lo = _split2(hn)
        rt_hi, rt_lo = _split2(rt_ref[...])
        logits = _dot(hn_hi, rt_hi) + (_dot(hn_hi, rt_lo) + _dot(hn_lo, rt_hi))
        lane = lax.broadcasted_iota(jnp.int32, logits.shape, 1)
        logits = jnp.where(lane < N_EXPERTS, logits, -jnp.inf)
        ex = jnp.exp(logits - jnp.max(logits, axis=1, keepdims=True))
        probs = ex / jnp.sum(ex, axis=1, keepdims=True)
        p1 = jnp.max(probs, axis=1, keepdims=True)
        i1 = jnp.min(jnp.where(probs == p1, lane, LANES), axis=1, keepdims=True)
        rest = jnp.where(lane == i1, -1.0, probs)
        p2 = jnp.max(rest, axis=1, keepdims=True)
        i2 = jnp.min(jnp.where(rest == p2, lane, LANES), axis=1, keepdims=True)
        tot = p1 + p2
        gate_s[...] = jnp.where(lane == i1, p1 / tot, 0.0) + jnp.where(lane == i2, p2 / tot, 0.0)
        acc_s[...] = jnp.zeros_like(acc_s)

    hn = hn_s[...]
    a = _dot(hn, w1_ref[0])
    b = _dot(hn, w3_ref[0])
    y = _dot((jax.nn.silu(a) * b).astype(BF16), w2_ref[0])
    lane = lax.broadcasted_iota(jnp.int32, gate_s.shape, 1)
    g = jnp.sum(jnp.where(lane == e, gate_s[...], 0.0), axis=1, keepdims=True)
    acc_s[...] += g * y

    @pl.when(jnp.logical_and(e == N_EXPERTS - 1, c == 1))
    def _():
        h5 = h4_s[...] + acc_s[...]
        ple = _dot(p_ref[...].astype(BF16), wple_ref[...]) * jax.nn.sigmoid(_dot(h5.astype(BF16), wgate_ref[...]))
        y_ref[...] = _rmsn(h5 + ple) * gfin_ref[...]


def _moe_layer(h3, attn, p, wo, gf, rt, w1, w3, w2, wple, wgate, gfin, tm):
    n = h3.shape[0]
    tm = min(tm, n)
    tile = pl.BlockSpec((tm, D_MODEL), lambda i, e, c: (i, 0))
    return pl.pallas_call(
        _moe_kernel,
        grid=(n // tm, N_EXPERTS, 2),
        in_specs=[tile, tile, pl.BlockSpec((tm, PLE_DIM), lambda i, e, c: (i, 0)),
                  _const_spec(wo.shape), _const_spec(gf.shape), _const_spec(rt.shape),
                  pl.BlockSpec((1, D_MODEL, FF_HALF), lambda i, e, c: (e, 0, c)),
                  pl.BlockSpec((1, D_MODEL, FF_HALF), lambda i, e, c: (e, 0, c)),
                  pl.BlockSpec((1, FF_HALF, D_MODEL), lambda i, e, c: (e, c, 0)),
                  _const_spec(wple.shape), _const_spec(wgate.shape), _const_spec(gfin.shape)],
        out_specs=tile,
        out_shape=jax.ShapeDtypeStruct((n, D_MODEL), F32),
        scratch_shapes=[pltpu.VMEM((tm, D_MODEL), F32), pltpu.VMEM((tm, D_MODEL), BF16),
                        pltpu.VMEM((tm, LANES), F32), pltpu.VMEM((tm, D_MODEL), F32)],
        compiler_params=_params(("arbitrary", "arbitrary", "arbitrary")),
        name="oproj_moe",
    )(h3, attn, p, wo, gf, rt, w1, w3, w2, wple, wgate, gfin)


S5_CHUNK = 256
TOKEN_TILE = 512


def kernel(x_prompt, x_sample, cache_k, cache_v, state_ssm_re, state_ssm_im, page_table, p_prompt, p_sample,
           rel_bias, norm_mix, norm_ffn, norm_kv, norm_final, ssm_lam_re, ssm_lam_im, ssm_log_dt, ssm_b_re,
           ssm_b_im, ssm_c_re, ssm_c_im, ssm_d, w_glu_a, w_glu_b, w_q, w_k, w_v, w_o, ffn_w1, ffn_w3, ffn_w2,
           router, moe_w1, moe_w3, moe_w2, w_ple, w_ple_gate):
    assert x_prompt.shape[0] == 1 and norm_mix.shape[0] == 2, "one prompt sequence, depth 2"
    t_len = x_prompt.shape[1]
    n_seq, n_tok, _ = x_sample.shape
    row = lambda v: v.reshape(1, -1).astype(F32)
    bf = lambda w: w.astype(BF16)

    bblk, cre, cim, tabs, a1 = _s5_tables(ssm_lam_re[0], ssm_lam_im[0], ssm_log_dt[0], ssm_b_re[0], ssm_b_im[0],
                                          ssm_c_re[0], ssm_c_im[0])
    g_mix0, g_mix1 = row(norm_mix[0]), row(norm_mix[1])
    g_ffn0, g_ffn1 = row(norm_ffn[0]), row(norm_ffn[1])
    g_kv, g_fin, d_skip = row(norm_kv), row(norm_final), row(ssm_d[0])
    wa, wb = bf(w_glu_a[0]), bf(w_glu_b[0])
    f_w1, f_w3, f_w2 = bf(ffn_w1[0]), bf(ffn_w3[0]), bf(ffn_w2[0])
    wple0, wple1 = bf(w_ple[0]), bf(w_ple[1])
    wgate0, wgate1 = bf(w_ple_gate[0]), bf(w_ple_gate[1])
    wk, wv, wq, wo = bf(w_k), bf(w_v), bf(w_q[0]), bf(w_o[0])
    m_w1, m_w3, m_w2 = bf(moe_w1[0]), bf(moe_w3[0]), bf(moe_w2[0])
    rt = jnp.pad(router[0].astype(F32), ((0, 0), (0, LANES - N_EXPERTS)))

    def ffn0(h1, p, with_kmean):
        return _ffn0(h1, p, g_ffn0, f_w1, f_w3, f_w2, wple0, wgate0, g_kv, g_mix1, wk, wv, wq,
                     TOKEN_TILE, with_kmean)

    def moe(h3, attn, p):
        return _moe_layer(h3, attn, p, wo, g_ffn1, rt, m_w1, m_w3, m_w2, wple1, wgate1, g_fin, TOKEN_TILE)

    h1, sre_p, sim_p = _s5_prompt(x_prompt[0].astype(F32), g_mix0, bblk, cre, cim, tabs, d_skip, wa, wb, S5_CHUNK)
    h3, k_p, v_p, q_p, kmean = ffn0(h1, p_prompt[0, 0], True)
    attn_p = _moba_prompt(q_p, k_p, v_p, kmean.reshape(-1, D_MODEL), rel_bias)
    y_p = moe(h3, attn_p, p_prompt[1, 0])

    n_s = n_seq * n_tok
    x_tm = x_sample.astype(F32).transpose(1, 0, 2).reshape(n_s, D_MODEL)
    h1s, sre_s, sim_s = _s5_sample(x_tm, g_mix0, bblk, cre, cim, a1,
                                   state_ssm_re[0].reshape(n_seq, N_STATE), state_ssm_im[0].reshape(n_seq, N_STATE),
                                   d_skip, wa, wb, n_seq, n_tok)
    h1s = h1s.reshape(n_tok, n_seq, D_MODEL).transpose(1, 0, 2).reshape(n_s, D_MODEL)
    h3s, k_s, v_s, q_s = ffn0(h1s, p_sample[0].reshape(n_s, PLE_DIM), False)
    attn_s = _moba_sample(q_s, k_s, v_s, cache_k, cache_v, page_table, rel_bias)
    y_s = moe(h3s, attn_s, p_sample[1].reshape(n_s, PLE_DIM))

    st = lambda s, nb: s.reshape(1, nb, N_GROUPS, SSM_STATE)
    hd = lambda a, *lead: a.reshape(*lead, N_HEADS, HEAD_DIM)
    return (y_p.reshape(1, t_len, D_MODEL), y_s.reshape(n_seq, n_tok, D_MODEL),
            hd(k_p, 1, t_len), hd(v_p, 1, t_len), st(sre_p, 1), st(sim_p, 1),
            hd(k_s, n_seq, n_tok), hd(v_s, n_seq, n_tok), st(sre_s, n_seq), st(sim_s, n_seq))
```

```python
import functools
import math

import jax
import jax.numpy as jnp
import numpy as np
from jax import lax
from jax.experimental import pallas as pl
from jax.experimental.pallas import tpu as pltpu

F32 = jnp.float32
BF16 = jnp.bfloat16

D_MODEL = 1024
SSM_GROUP = 16
N_GROUPS = D_MODEL // SSM_GROUP
SSM_STATE = 64
N_STATE = N_GROUPS * SSM_STATE
S5_BLOCKS = 4
S5_IN = D_MODEL // S5_BLOCKS
S5_ST = N_STATE // S5_BLOCKS
N_HEADS = 16
HEAD_DIM = 64
MOBA_BLOCK = 256
MOBA_TOPK = 3
PAGE_SIZE = 128
PAGES_PER_BLOCK = MOBA_BLOCK // PAGE_SIZE
REL_BUCKETS = 32
REL_MAX_DIST = 128
D_FF = 2816
FF_HALF = D_FF // 2
N_EXPERTS = 8
PLE_DIM = 256
EPS = 1e-6
NEG = -1e30
LOG2E = math.log2(math.e)
SUBLANES = 8
LANES = 128
VMEM_LIMIT = 56 * 1024 * 1024
MOE_VMEM_LIMIT = 60 * 1024 * 1024
PAGES_PER_STEP = 16
FAR_GROUP = 4
DEN_ROWS = 16


def _rmsn(x):
    return x * lax.rsqrt(jnp.mean(x * x, axis=-1, keepdims=True) + EPS)


def _dot(a, b):
    return jnp.dot(a, b, preferred_element_type=F32)


def _dot_nt(a, b):
    return lax.dot_general(a, b, (((1,), (1,)), ((), ())), preferred_element_type=F32)


def _split2(x):
    hi = x.astype(BF16)
    lo = (x - hi.astype(F32)).astype(BF16)
    return hi, lo


def _const_spec(shape):
    nd = len(shape)
    return pl.BlockSpec(shape, lambda *_: (0,) * nd, pipeline_mode=pl.Buffered(1))


def _params(sem, vmem=VMEM_LIMIT):
    return pltpu.CompilerParams(dimension_semantics=sem, vmem_limit_bytes=vmem)


def _s5_tables(lam_re, lam_im, log_dt, b_re, b_im, c_re, c_im):
    dt = jnp.exp(log_dt.astype(F32))[:, None]
    lr = lam_re.astype(F32)
    li = lam_im.astype(F32)
    mag = jnp.exp(lr * dt)
    a_re = mag * jnp.cos(li * dt)
    a_im = mag * jnp.sin(li * dt)
    den = lr * lr + li * li
    n_re = a_re - 1.0
    k_re = (n_re * lr + a_im * li) / den
    k_im = (a_im * lr - n_re * li) / den
    bb_re = k_re[..., None] * b_re - k_im[..., None] * b_im
    bb_im = k_re[..., None] * b_im + k_im[..., None] * b_re
    gpb = N_GROUPS // S5_BLOCKS
    eye = jnp.eye(gpb, dtype=F32)

    def pack_b(bb):
        t = bb.reshape(S5_BLOCKS, gpb, SSM_STATE, SSM_GROUP)
        t = jnp.einsum('jgpc,gh->jgchp', t, eye)
        return t.reshape(S5_BLOCKS, S5_IN, S5_ST)

    def pack_c(cc):
        t = cc.astype(F32).reshape(S5_BLOCKS, gpb, SSM_GROUP, SSM_STATE)
        t = jnp.einsum('jgcp,gh->jgphc', t, eye)
        return t.reshape(S5_BLOCKS, S5_ST, S5_IN)

    bblk = jnp.concatenate([pack_b(bb_re), pack_b(bb_im)], axis=-1).astype(BF16)
    cre = pack_c(c_re).astype(BF16)
    cim = (-pack_c(c_im)).astype(BF16)
    ar = a_re.reshape(1, N_STATE)
    ai = a_im.reshape(1, N_STATE)
    pw = [(ar, ai)]
    for _ in range(SUBLANES - 1):
        pr, pi = pw[-1]
        pw.append((pr * ar - pi * ai, pr * ai + pi * ar))
    row = jnp.arange(SUBLANES)[:, None]

    def masked(k):
        return (jnp.where(row >= k, pw[k - 1][0], 0.0), jnp.where(row >= k, pw[k - 1][1], 0.0))

    m1, m2, m4 = masked(1), masked(2), masked(4)
    a_r = jnp.concatenate([p[0] for p in pw], axis=0)
    a_i = jnp.concatenate([p[1] for p in pw], axis=0)
    tabs = jnp.stack([m1[0], m1[1], m2[0], m2[1], m4[0], m4[1], a_r, a_i])
    a1 = jnp.concatenate([ar, ai], axis=0)
    return bblk, cre, cim, tabs, a1


def _s5_out(x, u, y, d_ref, wa_ref, wb_ref):
    y = y + d_ref[...] * u
    hb = jax.nn.gelu(y).astype(BF16)
    return x + _dot(hb, wa_ref[...]) * jax.nn.sigmoid(_dot(hb, wb_ref[...]))


def _s5_prompt_kernel(x_ref, g_ref, bblk_ref, cre_ref, cim_ref, tabs_ref, d_ref, wa_ref, wb_ref,
                      h_ref, sre_ref, sim_ref, cr_s, ci_s, xr_s, xi_s):
    @pl.when(pl.program_id(0) == 0)
    def _():
        cr_s[...] = jnp.zeros_like(cr_s)
        ci_s[...] = jnp.zeros_like(ci_s)

    x = x_ref[...]
    u = _rmsn(x) * g_ref[...]
    ub = u.astype(BF16)
    n8 = x.shape[0] // SUBLANES
    ys = []
    for j in range(S5_BLOCKS):
        lo, hi = j * S5_ST, (j + 1) * S5_ST
        bu = _dot(ub[:, j * S5_IN:(j + 1) * S5_IN], bblk_ref[j])
        xr = bu[:, :S5_ST].reshape(n8, SUBLANES, S5_ST)
        xi = bu[:, S5_ST:].reshape(n8, SUBLANES, S5_ST)
        for k, sh in ((0, 1), (2, 2), (4, 4)):
            mr = tabs_ref[k, :, lo:hi]
            mi = tabs_ref[k + 1, :, lo:hi]
            rr = pltpu.roll(xr, sh, 1)
            ri = pltpu.roll(xi, sh, 1)
            xr, xi = xr + (mr * rr - mi * ri), xi + (mr * ri + mi * rr)
        a_r = tabs_ref[6, :, lo:hi]
        a_i = tabs_ref[7, :, lo:hi]
        pr = cr_s[:, lo:hi]
        pi = ci_s[:, lo:hi]
        for t in range(n8):
            br = jnp.broadcast_to(pr, (SUBLANES, S5_ST))
            bi = jnp.broadcast_to(pi, (SUBLANES, S5_ST))
            tr = xr[t] + (a_r * br - a_i * bi)
            ti = xi[t] + (a_r * bi + a_i * br)
            xr_s[t * SUBLANES:(t + 1) * SUBLANES, :] = tr
            xi_s[t * SUBLANES:(t + 1) * SUBLANES, :] = ti
            pr = tr[SUBLANES - 1:SUBLANES, :]
            pi = ti[SUBLANES - 1:SUBLANES, :]
        cr_s[:, lo:hi] = pr
        ci_s[:, lo:hi] = pi
        ys.append(_dot(xr_s[...].astype(BF16), cre_ref[j]) + _dot(xi_s[...].astype(BF16), cim_ref[j]))
    y = jnp.concatenate(ys, axis=1)
    h_ref[...] = _s5_out(x, u, y, d_ref, wa_ref, wb_ref)
    sre_ref[...] = cr_s[...]
    sim_ref[...] = ci_s[...]


def _s5_prompt(x, g, bblk, cre, cim, tabs, d, wa, wb, chunk):
    t_len = x.shape[0]
    chunk = min(chunk, t_len)
    return pl.pallas_call(
        _s5_prompt_kernel,
        grid=(t_len // chunk,),
        in_specs=[pl.BlockSpec((chunk, D_MODEL), lambda i: (i, 0)),
                  _const_spec(g.shape), _const_spec(bblk.shape), _const_spec(cre.shape), _const_spec(cim.shape),
                  _const_spec(tabs.shape), _const_spec(d.shape), _const_spec(wa.shape), _const_spec(wb.shape)],
        out_specs=[pl.BlockSpec((chunk, D_MODEL), lambda i: (i, 0)),
                   pl.BlockSpec((1, N_STATE), lambda i: (0, 0)),
                   pl.BlockSpec((1, N_STATE), lambda i: (0, 0))],
        out_shape=[jax.ShapeDtypeStruct((t_len, D_MODEL), F32),
                   jax.ShapeDtypeStruct((1, N_STATE), F32),
                   jax.ShapeDtypeStruct((1, N_STATE), F32)],
        scratch_shapes=[pltpu.VMEM((1, N_STATE), F32), pltpu.VMEM((1, N_STATE), F32),
                        pltpu.VMEM((chunk, S5_ST), F32), pltpu.VMEM((chunk, S5_ST), F32)],
        compiler_params=_params(("arbitrary",)),
        name="s5_prompt",
    )(x, g, bblk, cre, cim, tabs, d, wa, wb)


def _s5_sample_kernel(x_ref, g_ref, bblk_ref, cre_ref, cim_ref, a_ref, s0r_ref, s0i_ref, d_ref, wa_ref, wb_ref,
                      h_ref, sre_ref, sim_ref, xr_s, xi_s, *, n_seq, n_tok):
    x = x_ref[...]
    u = _rmsn(x) * g_ref[...]
    ub = u.astype(BF16)
    ys = []
    for j in range(S5_BLOCKS):
        lo, hi = j * S5_ST, (j + 1) * S5_ST
        bu = _dot(ub[:, j * S5_IN:(j + 1) * S5_IN], bblk_ref[j])
        ar = a_ref[0:1, lo:hi]
        ai = a_ref[1:2, lo:hi]
        sr = s0r_ref[:, lo:hi]
        si = s0i_ref[:, lo:hi]
        for t in range(n_tok):
            rows = bu[t * n_seq:(t + 1) * n_seq]
            sr, si = ar * sr - ai * si + rows[:, :S5_ST], ar * si + ai * sr + rows[:, S5_ST:]
            xr_s[t * n_seq:(t + 1) * n_seq, :] = sr
            xi_s[t * n_seq:(t + 1) * n_seq, :] = si
        sre_ref[:, lo:hi] = sr
        sim_ref[:, lo:hi] = si
        ys.append(_dot(xr_s[...].astype(BF16), cre_ref[j]) + _dot(xi_s[...].astype(BF16), cim_ref[j]))
    y = jnp.concatenate(ys, axis=1)
    h_ref[...] = _s5_out(x, u, y, d_ref, wa_ref, wb_ref)


def _s5_sample(x_tm, g, bblk, cre, cim, a1, s0r, s0i, d, wa, wb, n_seq, n_tok):
    n = n_seq * n_tok
    args = (x_tm, g, bblk, cre, cim, a1, s0r, s0i, d, wa, wb)
    return pl.pallas_call(
        functools.partial(_s5_sample_kernel, n_seq=n_seq, n_tok=n_tok),
        grid=(1,),
        in_specs=[_const_spec(a.shape) for a in args],
        out_specs=[_const_spec((n, D_MODEL)), _const_spec((n_seq, N_STATE)), _const_spec((n_seq, N_STATE))],
        out_shape=[jax.ShapeDtypeStruct((n, D_MODEL), F32),
                   jax.ShapeDtypeStruct((n_seq, N_STATE), F32),
                   jax.ShapeDtypeStruct((n_seq, N_STATE), F32)],
        scratch_shapes=[pltpu.VMEM((n, S5_ST), F32), pltpu.VMEM((n, S5_ST), F32)],
        compiler_params=_params(("arbitrary",)),
        name="s5_sample",
    )(*args)


def _ffn0_kernel(h1_ref, p_ref, gf_ref, w1_ref, w3_ref, w2_ref, wple_ref, wgate_ref, gkv_ref, gq_ref,
                 wk_ref, wv_ref, wq_ref, h3_ref, k_ref, v_ref, q_ref, *km_ref, n_blk):
    x = h1_ref[...]
    hn = (_rmsn(x) * gf_ref[...]).astype(BF16)
    h2 = x
    for c in range(2):
        lo, hi = c * FF_HALF, (c + 1) * FF_HALF
        a = _dot(hn, w1_ref[:, lo:hi])
        b = _dot(hn, w3_ref[:, lo:hi])
        h2 = h2 + _dot((jax.nn.silu(a) * b).astype(BF16), w2_ref[lo:hi, :])
    ple = _dot(p_ref[...].astype(BF16), wple_ref[...]) * jax.nn.sigmoid(_dot(h2.astype(BF16), wgate_ref[...]))
    h3 = h2 + ple
    h3_ref[...] = h3
    xn = _rmsn(h3)
    kvb = (xn * gkv_ref[...]).astype(BF16)
    qb = (xn * gq_ref[...]).astype(BF16)
    k = _dot(kvb, wk_ref[...])
    k_ref[...] = k
    v_ref[...] = _dot(kvb, wv_ref[...])
    q_ref[...] = _dot(qb, wq_ref[...])
    for b in range(n_blk):
        km_ref[0][b] = jnp.mean(k[b * MOBA_BLOCK:(b + 1) * MOBA_BLOCK], axis=0, keepdims=True)


def _ffn0(h1, p, gf, w1, w3, w2, wple, wgate, gkv, gq, wk, wv, wq, tm, with_kmean):
    n = h1.shape[0]
    tm = min(tm, n)
    n_blk = tm // MOBA_BLOCK if with_kmean else 0
    consts = (gf, w1, w3, w2, wple, wgate, gkv, gq, wk, wv, wq)
    tile = pl.BlockSpec((tm, D_MODEL), lambda i: (i, 0))
    out_specs = [tile, tile, tile, tile]
    out_shape = [jax.ShapeDtypeStruct((n, D_MODEL), F32)] * 4
    if with_kmean:
        out_specs.append(pl.BlockSpec((n_blk, 1, D_MODEL), lambda i: (i, 0, 0)))
        out_shape.append(jax.ShapeDtypeStruct((n // MOBA_BLOCK, 1, D_MODEL), F32))
    return pl.pallas_call(
        functools.partial(_ffn0_kernel, n_blk=n_blk),
        grid=(n // tm,),
        in_specs=[tile, pl.BlockSpec((tm, PLE_DIM), lambda i: (i, 0))] + [_const_spec(c.shape) for c in consts],
        out_specs=out_specs,
        out_shape=out_shape,
        compiler_params=_params(("arbitrary",)),
        name="ffn0_qkv",
    )(h1, p, *consts)


def _rel_bucket(dist):
    max_exact = REL_BUCKETS // 2
    d = jnp.maximum(dist, 0)
    large = max_exact + (jnp.log(jnp.maximum(d, max_exact).astype(F32) / max_exact)
                         / math.log(REL_MAX_DIST / max_exact) * (REL_BUCKETS - max_exact)).astype(jnp.int32)
    large = jnp.minimum(large, REL_BUCKETS - 1)
    return jnp.where(d < max_exact, d, large)


def _bias_by_distance(rel_bias, n):
    return rel_bias[_rel_bucket(jnp.arange(n))]


FAR_DIST = REL_MAX_DIST


def _top3_rows(cur, rows, n_rows):
    picked = jnp.zeros(cur.shape, jnp.bool_)
    for _ in range(MOBA_TOPK):
        m = jnp.max(cur, axis=0, keepdims=True)
        idx = jnp.min(jnp.where(cur == m, rows, n_rows), axis=0, keepdims=True)
        hit = rows == idx
        picked = jnp.logical_or(picked, hit)
        cur = jnp.where(hit, -jnp.inf, cur)
    return picked


def _moba_prompt_kernel(q_ref, k_ref, v_ref, km_ref, bias_ref, o_ref, sel_s, s_s, smax_s, *, n_blk):
    own = pl.program_id(1)
    q = q_ref[0]
    tq = q.shape[0]
    q_hi, q_lo = _split2(q)
    km_hi, km_lo = _split2(km_ref[0])
    sc = _dot_nt(km_hi, q_hi) + (_dot_nt(km_hi, q_lo) + _dot_nt(km_lo, q_hi))
    rows = lax.broadcasted_iota(jnp.int32, sc.shape, 0)
    past = rows < own
    picked = _top3_rows(jnp.where(past, sc, NEG), rows, n_blk)
    sel = jnp.where(jnp.logical_and(picked, past), 1.0, 0.0)
    for n in range(n_blk):
        sel_s[n] = sel[n:n + 1, :]

    qb = (q * (HEAD_DIM ** -0.5 * LOG2E)).astype(BF16)
    grp = FAR_GROUP

    def softmax_step(carry, s3, smax, on3, vs):
        m_run, acc = carry
        m_new = jnp.maximum(m_run, jnp.max(jnp.where(on3, smax, NEG), axis=0))
        alpha = jnp.exp2(m_run - m_new)
        pb = jnp.exp2(s3 - jnp.where(on3, m_new[None], -NEG)).astype(BF16)
        acc = alpha * acc
        for j, vj in enumerate(vs):
            acc = acc + _dot(vj, pb[j])
        return m_new, acc

    n_far = jnp.maximum(own - 1, 0)
    blk_iota = lax.broadcasted_iota(jnp.int32, (grp, 1, tq), 0)

    def produce(g, slot):
        start = jnp.minimum(g * grp, n_blk - grp)
        for j in range(grp):
            s = _dot_nt(k_ref[0, start + j], qb)
            s_s[slot, j] = s
            smax_s[slot, j] = jnp.max(s, axis=0, keepdims=True)

    def consume(g, slot, carry):
        start = jnp.minimum(g * grp, n_blk - grp)
        on3 = jnp.logical_and(sel_s[pl.ds(start, grp)] > 0.0, blk_iota + g * grp < n_far)
        return softmax_step(carry, s_s[slot], smax_s[slot], on3, [v_ref[0, start + j] for j in range(grp)])

    def far_body(i, carry):
        produce(2 * i + 1, 1)
        carry = consume(2 * i, 0, carry)
        produce(2 * i + 2, 0)
        return consume(2 * i + 1, 1, carry)

    init = (jnp.full((1, tq), NEG, F32), jnp.zeros((HEAD_DIM + DEN_ROWS, tq), F32))
    produce(0, 0)
    n_grp = (n_far + grp - 1) // grp
    carry = lax.fori_loop(0, (n_grp + 1) // 2, far_body, init)
    prev = jnp.maximum(own - 1, 0)
    k2 = jnp.concatenate([k_ref[0, prev], k_ref[0, own]], axis=0)
    s2 = _dot_nt(k2, qb).reshape(2, MOBA_BLOCK, tq) + bias_ref[0]
    on2 = jnp.concatenate([sel_s[pl.ds(prev, 1)] > 0.0, jnp.ones((1, 1, tq), jnp.bool_)], axis=0)
    _, acc = softmax_step(carry, s2, jnp.max(s2, axis=1, keepdims=True), on2, [v_ref[0, prev], v_ref[0, own]])
    o_ref[0] = acc[:HEAD_DIM] / acc[HEAD_DIM:HEAD_DIM + 1]


def _moba_prompt(q, k, v, kmean, rel_bias):
    t_len = q.shape[0]
    n_blk = t_len // MOBA_BLOCK
    qh = q.reshape(t_len, N_HEADS, HEAD_DIM).transpose(1, 0, 2)
    kh = k.astype(BF16).reshape(n_blk, MOBA_BLOCK, N_HEADS, HEAD_DIM).transpose(2, 0, 1, 3)
    vt = v.astype(BF16).reshape(n_blk, MOBA_BLOCK, N_HEADS, HEAD_DIM).transpose(2, 0, 3, 1)
    vt = jnp.concatenate([vt, jnp.ones((N_HEADS, n_blk, DEN_ROWS, MOBA_BLOCK), BF16)], axis=2)
    kmh = kmean.reshape(n_blk, N_HEADS, HEAD_DIM).transpose(1, 0, 2)
    nb2 = 2 * MOBA_BLOCK
    bd = ((_bias_by_distance(rel_bias, nb2) - rel_bias[REL_BUCKETS - 1]) * LOG2E).T

    def toeplitz(a):
        flat = jnp.tile(a, (1, MOBA_BLOCK))[:, :MOBA_BLOCK * (nb2 - 1)]
        return flat.reshape(N_HEADS, MOBA_BLOCK, nb2 - 1)[:, :, :MOBA_BLOCK]

    own_t = toeplitz(jnp.concatenate([bd[:, :MOBA_BLOCK], jnp.full((N_HEADS, MOBA_BLOCK), NEG, F32)], axis=1))
    prev_t = toeplitz(jnp.concatenate([bd[:, MOBA_BLOCK:], bd[:, :MOBA_BLOCK]], axis=1))
    bias = jnp.stack([prev_t, own_t], axis=1)
    assert n_blk % FAR_GROUP == 0
    out = pl.pallas_call(
        functools.partial(_moba_prompt_kernel, n_blk=n_blk),
        grid=(N_HEADS, n_blk),
        in_specs=[pl.BlockSpec((1, MOBA_BLOCK, HEAD_DIM), lambda h, i: (h, i, 0)),
                  pl.BlockSpec((1, n_blk, MOBA_BLOCK, HEAD_DIM), lambda h, i: (h, 0, 0, 0)),
                  pl.BlockSpec((1, n_blk, HEAD_DIM + DEN_ROWS, MOBA_BLOCK), lambda h, i: (h, 0, 0, 0)),
                  pl.BlockSpec((1, n_blk, HEAD_DIM), lambda h, i: (h, 0, 0)),
                  pl.BlockSpec((1, 2, MOBA_BLOCK, MOBA_BLOCK), lambda h, i: (h, 0, 0, 0))],
        out_specs=pl.BlockSpec((1, HEAD_DIM, MOBA_BLOCK), lambda h, i: (h, 0, i)),
        out_shape=jax.ShapeDtypeStruct((N_HEADS, HEAD_DIM, t_len), F32),
        scratch_shapes=[pltpu.VMEM((n_blk, 1, MOBA_BLOCK), F32),
                        pltpu.VMEM((2, FAR_GROUP, MOBA_BLOCK, MOBA_BLOCK), F32),
                        pltpu.VMEM((2, FAR_GROUP, 1, MOBA_BLOCK), F32)],
        compiler_params=_params(("arbitrary", "arbitrary")),
        name="moba_prompt",
    )(qh, kh, vt, kmh, bias)
    return out.transpose(2, 0, 1).reshape(t_len, D_MODEL)


def _page_mean_kernel(pt_ref, *refs):
    del pt_ref
    ones_ref, o_ref = refs[-2], refs[-1]
    for b in range(PAGES_PER_STEP // PAGES_PER_BLOCK):
        x = (refs[2 * b][0] + refs[2 * b + 1][0]).reshape(D_MODEL, PAGE_SIZE)
        hi, lo = _split2(x)
        r = _dot_nt(ones_ref[...], hi) + _dot_nt(ones_ref[...], lo)
        o_ref[b:b + 1, :] = r[0:1] * (1.0 / MOBA_BLOCK)


def _block_means(cache_kt, pt_flat):
    n_pages_total = pt_flat.shape[0]
    steps = n_pages_total // PAGES_PER_STEP
    blocks_per_step = PAGES_PER_STEP // PAGES_PER_BLOCK
    ones = jnp.ones((SUBLANES, PAGE_SIZE), BF16)

    def page_spec(r):
        return pl.BlockSpec((1, N_HEADS, HEAD_DIM, PAGE_SIZE),
                            lambda i, pt: (pt[i * PAGES_PER_STEP + r], 0, 0, 0))

    return pl.pallas_call(
        _page_mean_kernel,
        grid_spec=pltpu.PrefetchScalarGridSpec(
            num_scalar_prefetch=1, grid=(steps,),
            in_specs=[page_spec(r) for r in range(PAGES_PER_STEP)] + [_const_spec(ones.shape)],
            out_specs=pl.BlockSpec((blocks_per_step, D_MODEL), lambda i, pt: (i, 0))),
        out_shape=jax.ShapeDtypeStruct((n_pages_total // PAGES_PER_BLOCK, D_MODEL), F32),
        compiler_params=_params(("arbitrary",)),
        name="page_means",
    )(pt_flat, *([cache_kt] * PAGES_PER_STEP), ones)


def _sample_select_kernel(q_ref, pm_ref, ones_ref, idx_ref, *, n_tok, n_past):
    pm = pm_ref[...]
    blk = lax.broadcasted_iota(jnp.int32, (n_past, N_HEADS, LANES), 0)
    for t in range(n_tok):
        prod = (pm * q_ref[0, t]).reshape(n_past * N_HEADS, HEAD_DIM)
        p1 = prod.astype(BF16)
        r1 = prod - p1.astype(F32)
        p2 = r1.astype(BF16)
        p3 = (r1 - p2.astype(F32)).astype(BF16)
        sc = _dot(p1, ones_ref[...]) + (_dot(p2, ones_ref[...]) + _dot(p3, ones_ref[...]))
        cur = sc.reshape(n_past, N_HEADS, LANES)
        for i in range(MOBA_TOPK):
            m = jnp.max(cur, axis=0, keepdims=True)
            idx = jnp.min(jnp.where(cur == m, blk, n_past), axis=0)
            idx_ref[0, t, i] = idx
            cur = jnp.where(blk == idx[None], -jnp.inf, cur)


def _sample_select(q4, pm, n_past):
    n_seq, n_tok = q4.shape[:2]
    ones = jnp.ones((HEAD_DIM, LANES), BF16)
    return pl.pallas_call(
        functools.partial(_sample_select_kernel, n_tok=n_tok, n_past=n_past),
        grid=(n_seq,),
        in_specs=[pl.BlockSpec((1, n_tok, N_HEADS, HEAD_DIM), lambda b: (b, 0, 0, 0)),
                  pl.BlockSpec((n_past, N_HEADS, HEAD_DIM), lambda b: (b, 0, 0)),
                  _const_spec(ones.shape)],
        out_specs=pl.BlockSpec((1, n_tok, MOBA_TOPK, N_HEADS, LANES), lambda b: (b, 0, 0, 0, 0)),
        out_shape=jax.ShapeDtypeStruct((n_seq, n_tok, MOBA_TOPK, N_HEADS, LANES), jnp.int32),
        compiler_params=_params(("arbitrary",)),
        name="sample_select",
    )(q4, pm, ones)


N_SEL_PAGES = MOBA_TOPK * PAGES_PER_BLOCK


def _sample_attn_kernel(sel_ref, pt_ref, ck_ref, cv_ref, q_ref, kn_ref, vn_ref, near_ref, far_ref, ownb_ref,
                        o_ref, kbuf, vbuf, sems, *, n_tok, n_past, n_pages):
    b = pl.program_id(0)
    t = pl.program_id(1)

    def head_copies(h, slot):
        base = ((b * n_tok + t) * N_HEADS + h) * MOBA_TOPK
        cps = []
        for s in range(N_SEL_PAGES):
            blk = sel_ref[base + s // PAGES_PER_BLOCK]
            page = pt_ref[b * n_pages + blk * PAGES_PER_BLOCK + s % PAGES_PER_BLOCK]
            cps.append(pltpu.make_async_copy(ck_ref.at[page, h], kbuf.at[slot, s], sems.at[0, slot, s]))
            cps.append(pltpu.make_async_copy(cv_ref.at[page, h], vbuf.at[slot, s], sems.at[1, slot, s]))
        return cps

    for cp in head_copies(0, 0):
        cp.start()
    for h in range(N_HEADS):
        slot = h % 2
        if h + 1 < N_HEADS:
            for cp in head_copies(h + 1, 1 - slot):
                cp.start()
        for cp in head_copies(h, slot):
            cp.wait()
        qs = q_ref[0, 0, h:h + 1, :] * (HEAD_DIM ** -0.5)
        q8 = jnp.broadcast_to(qs, (SUBLANES, HEAD_DIM)).astype(BF16)
        base = ((b * n_tok + t) * N_HEADS + h) * MOBA_TOPK
        logits = []
        for s in range(N_SEL_PAGES):
            lg = _dot(q8, kbuf[slot, s].astype(BF16))[0:1]
            blk = sel_ref[base + s // PAGES_PER_BLOCK]
            is_near = jnp.logical_and(blk == n_past - 1, (s % PAGES_PER_BLOCK) == PAGES_PER_BLOCK - 1)
            logits.append(lg + jnp.where(is_near, near_ref[t * N_HEADS + h], far_ref[h]))
        kn = kn_ref[0, :, h, :]
        vn = vn_ref[0, :, h, :]
        l_own = jnp.sum(kn * qs, axis=1, keepdims=True) + ownb_ref[t * N_HEADS + h]
        m = jnp.max(l_own, axis=0, keepdims=True)
        for lg in logits:
            m = jnp.maximum(m, jnp.max(lg, axis=1, keepdims=True))
        p_own = jnp.exp(l_own - m)
        den = jnp.sum(p_own, axis=0, keepdims=True)
        o = jnp.sum(p_own * vn, axis=0, keepdims=True)
        for s, lg in enumerate(logits):
            p = jnp.exp(lg - m)
            den = den + jnp.sum(p, axis=1, keepdims=True)
            p8 = jnp.broadcast_to(p, (SUBLANES, PAGE_SIZE)).astype(BF16)
            o = o + _dot_nt(p8, vbuf[slot, s].astype(BF16))[0:1]
        o_ref[0, 0, h:h + 1, :] = o / den


def _moba_sample(q, k_new, v_new, cache_k, cache_v, page_table, rel_bias):
    n_seq, n_pages = page_table.shape
    n_tok = q.shape[0] // n_seq
    past_len = n_pages * PAGE_SIZE
    assert past_len % MOBA_BLOCK == 0, "own block must start at the first new token"
    n_past = past_len // MOBA_BLOCK
    assert n_past >= MOBA_TOPK and (n_seq * n_pages) % PAGES_PER_STEP == 0
    pt_flat = page_table.reshape(-1)
    cache_kt = cache_k.transpose(0, 2, 3, 1)
    cache_vt = cache_v.transpose(0, 2, 3, 1)
    pm = _block_means(cache_kt, pt_flat).reshape(-1, N_HEADS, HEAD_DIM)
    q4 = q.reshape(n_seq, n_tok, N_HEADS, HEAD_DIM)
    idx = _sample_select(q4, pm, n_past)
    sel = idx[..., 0].transpose(0, 1, 3, 2).reshape(-1)
    assert PAGE_SIZE + 1 >= FAR_DIST
    bd = _bias_by_distance(rel_bias, PAGE_SIZE + n_tok)
    tt = jnp.arange(n_tok)
    near = bd[PAGE_SIZE + tt[:, None] - jnp.arange(PAGE_SIZE)[None, :]]
    near = near.transpose(0, 2, 1).reshape(n_tok * N_HEADS, 1, PAGE_SIZE)
    far = jnp.broadcast_to(rel_bias[REL_BUCKETS - 1][:, None, None], (N_HEADS, 1, PAGE_SIZE))
    dd = tt[:, None] - tt[None, :]
    ownb = jnp.where((dd >= 0)[..., None], bd[jnp.maximum(dd, 0)], NEG)
    ownb = ownb.transpose(0, 2, 1).reshape(n_tok * N_HEADS, n_tok, 1)
    kn4 = k_new.reshape(n_seq, n_tok, N_HEADS, HEAD_DIM)
    vn4 = v_new.reshape(n_seq, n_tok, N_HEADS, HEAD_DIM)
    tok_spec = pl.BlockSpec((1, 1, N_HEADS, HEAD_DIM), lambda b, t, *_: (b, t, 0, 0))
    new_spec = pl.BlockSpec((1, n_tok, N_HEADS, HEAD_DIM), lambda b, t, *_: (b, 0, 0, 0))
    out = pl.pallas_call(
        functools.partial(_sample_attn_kernel, n_tok=n_tok, n_past=n_past, n_pages=n_pages),
        grid_spec=pltpu.PrefetchScalarGridSpec(
            num_scalar_prefetch=2, grid=(n_seq, n_tok),
            in_specs=[pl.BlockSpec(memory_space=pl.ANY), pl.BlockSpec(memory_space=pl.ANY),
                      tok_spec, new_spec, new_spec,
                      _const_spec(near.shape), _const_spec(far.shape), _const_spec(ownb.shape)],
            out_specs=tok_spec,
            scratch_shapes=[pltpu.VMEM((2, N_SEL_PAGES, HEAD_DIM, PAGE_SIZE), F32),
                            pltpu.VMEM((2, N_SEL_PAGES, HEAD_DIM, PAGE_SIZE), F32),
                            pltpu.SemaphoreType.DMA((2, 2, N_SEL_PAGES))]),
        out_shape=jax.ShapeDtypeStruct((n_seq, n_tok, N_HEADS, HEAD_DIM), F32),
        compiler_params=_params(("arbitrary", "arbitrary")),
        name="moba_sample",
    )(sel, pt_flat, cache_kt, cache_vt, q4, kn4, vn4, near, far, ownb)
    return out.reshape(n_seq * n_tok, D_MODEL)


def _moe_kernel(h3_ref, at_ref, p_ref, wo_ref, gf_ref, rtt_ref, utri_ref, w1_ref, w3_ref, w2_ref, wple_ref,
                wgate_ref, gfin_ref, y_ref, hn_s, grow_s, rrow_s, rcol_s, cnt_s, x_s, yc_s, *, rows):
    e = pl.program_id(1)
    half = pl.program_id(2)
    tc = hn_s.shape[0]

    slab = min(tc, MOE_SLAB)
    slabs = [slice(s, s + slab) for s in range(0, tc, slab)]

    def stream_in(sl):
        return h3_ref[sl, :] + _dot(at_ref[sl, :].astype(BF16), wo_ref[...])

    @pl.when(jnp.logical_and(e == 0, half == 0))
    def _():
        rt_hi, rt_lo = _split2(rtt_ref[...])
        lts = []
        for sl in slabs:
            hn = _rmsn(stream_in(sl)) * gf_ref[...]
            hn_s[sl, :] = hn.astype(BF16)
            hn_hi, hn_lo = _split2(hn)
            lts.append(_dot_nt(rt_hi, hn_hi) + (_dot_nt(rt_hi, hn_lo) + _dot_nt(rt_lo, hn_hi)))
            y_ref[sl, :] = jnp.zeros((slab, D_MODEL), F32)
        lt = jnp.concatenate(lts, axis=1)
        ex = jnp.exp(lt - jnp.max(lt, axis=0, keepdims=True))
        probs = ex / jnp.sum(ex, axis=0, keepdims=True)
        row = lax.broadcasted_iota(jnp.int32, probs.shape, 0)
        p1 = jnp.max(probs, axis=0, keepdims=True)
        i1 = jnp.min(jnp.where(probs == p1, row, N_EXPERTS), axis=0, keepdims=True)
        rest = jnp.where(row == i1, -1.0, probs)
        p2 = jnp.max(rest, axis=0, keepdims=True)
        i2 = jnp.min(jnp.where(rest == p2, row, N_EXPERTS), axis=0, keepdims=True)
        tot = p1 + p2
        gates = jnp.where(row == i1, p1 / tot, 0.0) + jnp.where(row == i2, p2 / tot, 0.0)
        mem = jnp.where(jnp.logical_or(row == i1, row == i2), 1.0, 0.0)
        rank = jnp.where(mem > 0.0, _dot(mem.astype(BF16), utri_ref[...]), -1.0)
        for k in range(N_EXPERTS):
            grow_s[k] = gates[k:k + 1]
            rrow_s[k] = rank[k:k + 1]
            cnt_s[k] = jnp.sum(mem[k:k + 1]).astype(jnp.int32)
        rcol_s[...] = rank.T

    n_tiles = (cnt_s[e] + rows - 1) // rows

    def one_hot_rows(j):
        r_iota = lax.broadcasted_iota(jnp.int32, (rows, tc), 0) + j * rows
        return jnp.where(rrow_s[e] == r_iota.astype(F32), 1.0, 0.0)

    @pl.when(half == 0)
    def _():
        def compact(j, _):
            x_s[j] = _dot(one_hot_rows(j).astype(BF16), hn_s[...]).astype(BF16)
            return 0
        lax.fori_loop(0, n_tiles, compact, 0)

    def expert_tile(j, _):
        x = x_s[j]
        a = _dot(x, w1_ref[0])
        b = _dot(x, w3_ref[0])
        y = _dot((jax.nn.silu(a) * b).astype(BF16), w2_ref[0])

        @pl.when(half == 0)
        def _():
            yc_s[j] = y

        @pl.when(half == 1)
        def _():
            g_rows = jnp.sum(one_hot_rows(j) * grow_s[e], axis=1, keepdims=True)
            hi, lo = _split2(g_rows * (yc_s[j] + y))
            lane_e = lax.broadcasted_iota(jnp.int32, rcol_s.shape, 1)
            rank_col = jnp.sum(jnp.where(lane_e == e, rcol_s[...], 0.0), axis=1, keepdims=True)
            c_iota = lax.broadcasted_iota(jnp.int32, (tc, rows), 1) + j * rows
            scat = jnp.where(rank_col == c_iota.astype(F32), 1.0, 0.0).astype(BF16)
            for sl in slabs:
                y_ref[sl, :] += _dot(scat[sl, :], hi) + _dot(scat[sl, :], lo)
        return 0

    lax.fori_loop(0, n_tiles, expert_tile, 0)

    @pl.when(jnp.logical_and(e == N_EXPERTS - 1, half == 1))
    def _():
        for sl in slabs:
            h5 = stream_in(sl) + y_ref[sl, :]
            ple = (_dot(p_ref[sl, :].astype(BF16), wple_ref[...])
                   * jax.nn.sigmoid(_dot(h5.astype(BF16), wgate_ref[...])))
            y_ref[sl, :] = _rmsn(h5 + ple) * gfin_ref[...]


def _moe_layer(h3, attn, p, wo, gf, rtt, w1, w3, w2, wple, wgate, gfin, chunk, rows):
    n = h3.shape[0]
    tc = min(chunk, n)
    rows = min(rows, tc)
    max_tiles = -(-tc // rows)
    utri = jnp.triu(jnp.ones((tc, tc), BF16), k=1)
    tile = pl.BlockSpec((tc, D_MODEL), lambda i, e, c: (i, 0), pipeline_mode=pl.Buffered(1))
    return pl.pallas_call(
        functools.partial(_moe_kernel, rows=rows),
        grid=(n // tc, N_EXPERTS, 2),
        in_specs=[tile, tile, pl.BlockSpec((tc, PLE_DIM), lambda i, e, c: (i, 0), pipeline_mode=pl.Buffered(1)),
                  _const_spec(wo.shape), _const_spec(gf.shape), _const_spec(rtt.shape), _const_spec(utri.shape),
                  pl.BlockSpec((1, D_MODEL, FF_HALF), lambda i, e, c: (e, 0, c)),
                  pl.BlockSpec((1, D_MODEL, FF_HALF), lambda i, e, c: (e, 0, c)),
                  pl.BlockSpec((1, FF_HALF, D_MODEL), lambda i, e, c: (e, c, 0)),
                  _const_spec(wple.shape), _const_spec(wgate.shape), _const_spec(gfin.shape)],
        out_specs=pl.BlockSpec((tc, D_MODEL), lambda i, e, c: (i, 0)),
        out_shape=jax.ShapeDtypeStruct((n, D_MODEL), F32),
        scratch_shapes=[pltpu.VMEM((tc, D_MODEL), BF16),
                        pltpu.VMEM((N_EXPERTS, 1, tc), F32), pltpu.VMEM((N_EXPERTS, 1, tc), F32),
                        pltpu.VMEM((tc, N_EXPERTS), F32), pltpu.SMEM((N_EXPERTS,), jnp.int32),
                        pltpu.VMEM((max_tiles, rows, D_MODEL), BF16), pltpu.VMEM((max_tiles, rows, D_MODEL), F32)],
        compiler_params=_params(("arbitrary", "arbitrary", "arbitrary"), vmem=MOE_VMEM_LIMIT),
        name="oproj_moe",
    )(h3, attn, p, wo, gf, rtt, utri, w1, w3, w2, wple, wgate, gfin)


S5_CHUNK = 256
TOKEN_TILE = 512
MOE_CHUNK = 1024
MOE_ROWS = 384
MOE_SLAB = 256


def kernel(x_prompt, x_sample, cache_k, cache_v, state_ssm_re, state_ssm_im, page_table, p_prompt, p_sample,
           rel_bias, norm_mix, norm_ffn, norm_kv, norm_final, ssm_lam_re, ssm_lam_im, ssm_log_dt, ssm_b_re,
           ssm_b_im, ssm_c_re, ssm_c_im, ssm_d, w_glu_a, w_glu_b, w_q, w_k, w_v, w_o, ffn_w1, ffn_w3, ffn_w2,
           router, moe_w1, moe_w3, moe_w2, w_ple, w_ple_gate):
    assert x_prompt.shape[0] == 1 and norm_mix.shape[0] == 2, "one prompt sequence, depth 2"
    t_len = x_prompt.shape[1]
    n_seq, n_tok, _ = x_sample.shape
    row = lambda v: v.reshape(1, -1).astype(F32)
    bf = lambda w: w.astype(BF16)

    bblk, cre, cim, tabs, a1 = _s5_tables(ssm_lam_re[0], ssm_lam_im[0], ssm_log_dt[0], ssm_b_re[0], ssm_b_im[0],
                                          ssm_c_re[0], ssm_c_im[0])
    g_mix0, g_mix1 = row(norm_mix[0]), row(norm_mix[1])
    g_ffn0, g_ffn1 = row(norm_ffn[0]), row(norm_ffn[1])
    g_kv, g_fin, d_skip = row(norm_kv), row(norm_final), row(ssm_d[0])
    wa, wb = bf(w_glu_a[0]), bf(w_glu_b[0])
    f_w1, f_w3, f_w2 = bf(ffn_w1[0]), bf(ffn_w3[0]), bf(ffn_w2[0])
    wple0, wple1 = bf(w_ple[0]), bf(w_ple[1])
    wgate0, wgate1 = bf(w_ple_gate[0]), bf(w_ple_gate[1])
    wk, wv, wq, wo = bf(w_k), bf(w_v), bf(w_q[0]), bf(w_o[0])
    m_w1, m_w3, m_w2 = bf(moe_w1[0]), bf(moe_w3[0]), bf(moe_w2[0])
    rtt = router[0].astype(F32).T

    def ffn0(h1, p, with_kmean):
        return _ffn0(h1, p, g_ffn0, f_w1, f_w3, f_w2, wple0, wgate0, g_kv, g_mix1, wk, wv, wq,
                     TOKEN_TILE, with_kmean)

    def moe(h3, attn, p):
        return _moe_layer(h3, attn, p, wo, g_ffn1, rtt, m_w1, m_w3, m_w2, wple1, wgate1, g_fin,
                          MOE_CHUNK, MOE_ROWS)

    h1, sre_p, sim_p = _s5_prompt(x_prompt[0].astype(F32), g_mix0, bblk, cre, cim, tabs, d_skip, wa, wb, S5_CHUNK)
    h3, k_p, v_p, q_p, kmean = ffn0(h1, p_prompt[0, 0], True)
    attn_p = _moba_prompt(q_p, k_p, v_p, kmean.reshape(-1, D_MODEL), rel_bias)
    y_p = moe(h3, attn_p, p_prompt[1, 0])

    n_s = n_seq * n_tok
    x_tm = x_sample.astype(F32).transpose(1, 0, 2).reshape(n_s, D_MODEL)
    h1s, sre_s, sim_s = _s5_sample(x_tm, g_mix0, bblk, cre, cim, a1,
                                   state_ssm_re[0].reshape(n_seq, N_STATE), state_ssm_im[0].reshape(n_seq, N_STATE),
                                   d_skip, wa, wb, n_seq, n_tok)
    h1s = h1s.reshape(n_tok, n_seq, D_MODEL).transpose(1, 0, 2).reshape(n_s, D_MODEL)
    h3s, k_s, v_s, q_s = ffn0(h1s, p_sample[0].reshape(n_s, PLE_DIM), False)
    attn_s = _moba_sample(q_s, k_s, v_s, cache_k, cache_v, page_table, rel_bias)
    y_s = moe(h3s, attn_s, p_sample[1].reshape(n_s, PLE_DIM))

    st = lambda s, nb: s.reshape(1, nb, N_GROUPS, SSM_STATE)
    hd = lambda a, *lead: a.reshape(*lead, N_HEADS, HEAD_DIM)
    return (y_p.reshape(1, t_len, D_MODEL), y_s.reshape(n_seq, n_tok, D_MODEL),
            hd(k_p, 1, t_len), hd(v_p, 1, t_len), st(sre_p, 1), st(sim_p, 1),
            hd(k_s, n_seq, n_tok), hd(v_s, n_seq, n_tok), st(sre_s, n_seq), st(sim_s, n_seq))
```

```python
import functools
import math

import jax
import jax.numpy as jnp
import numpy as np
from jax import lax
from jax.experimental import pallas as pl
from jax.experimental.pallas import tpu as pltpu

F32 = jnp.float32
BF16 = jnp.bfloat16

D_MODEL = 1024
SSM_GROUP = 16
N_GROUPS = D_MODEL // SSM_GROUP
SSM_STATE = 64
N_STATE = N_GROUPS * SSM_STATE
S5_BLOCKS = 4
S5_IN = D_MODEL // S5_BLOCKS
S5_ST = N_STATE // S5_BLOCKS
N_HEADS = 16
HEAD_DIM = 64
MOBA_BLOCK = 256
MOBA_TOPK = 3
PAGE_SIZE = 128
PAGES_PER_BLOCK = MOBA_BLOCK // PAGE_SIZE
REL_BUCKETS = 32
REL_MAX_DIST = 128
D_FF = 2816
FF_HALF = D_FF // 2
N_EXPERTS = 8
PLE_DIM = 256
EPS = 1e-6
NEG = -1e30
LOG2E = math.log2(math.e)
SUBLANES = 8
LANES = 128
VMEM_LIMIT = 56 * 1024 * 1024
MOE_VMEM_LIMIT = 60 * 1024 * 1024
PAGES_PER_STEP = 16
FAR_GROUP = 4
DEN_ROWS = 16


def _rmsn(x):
    return x * lax.rsqrt(jnp.mean(x * x, axis=-1, keepdims=True) + EPS)


def _dot(a, b):
    return jnp.dot(a, b, preferred_element_type=F32)


def _dot_nt(a, b):
    return lax.dot_general(a, b, (((1,), (1,)), ((), ())), preferred_element_type=F32)


def _split2(x):
    hi = x.astype(BF16)
    lo = (x - hi.astype(F32)).astype(BF16)
    return hi, lo


def _const_spec(shape):
    nd = len(shape)
    return pl.BlockSpec(shape, lambda *_: (0,) * nd, pipeline_mode=pl.Buffered(1))


def _params(sem, vmem=VMEM_LIMIT):
    return pltpu.CompilerParams(dimension_semantics=sem, vmem_limit_bytes=vmem)


def _s5_tables(lam_re, lam_im, log_dt, b_re, b_im, c_re, c_im):
    dt = jnp.exp(log_dt.astype(F32))[:, None]
    lr = lam_re.astype(F32)
    li = lam_im.astype(F32)
    mag = jnp.exp(lr * dt)
    a_re = mag * jnp.cos(li * dt)
    a_im = mag * jnp.sin(li * dt)
    den = lr * lr + li * li
    n_re = a_re - 1.0
    k_re = (n_re * lr + a_im * li) / den
    k_im = (a_im * lr - n_re * li) / den
    bb_re = k_re[..., None] * b_re - k_im[..., None] * b_im
    bb_im = k_re[..., None] * b_im + k_im[..., None] * b_re
    gpb = N_GROUPS // S5_BLOCKS
    eye = jnp.eye(gpb, dtype=F32)

    def pack_b(bb):
        t = bb.reshape(S5_BLOCKS, gpb, SSM_STATE, SSM_GROUP)
        t = jnp.einsum('jgpc,gh->jgchp', t, eye)
        return t.reshape(S5_BLOCKS, S5_IN, S5_ST)

    def pack_c(cc):
        t = cc.astype(F32).reshape(S5_BLOCKS, gpb, SSM_GROUP, SSM_STATE)
        t = jnp.einsum('jgcp,gh->jgphc', t, eye)
        return t.reshape(S5_BLOCKS, S5_ST, S5_IN)

    bblk = jnp.concatenate([pack_b(bb_re), pack_b(bb_im)], axis=-1).astype(BF16)
    cre = pack_c(c_re).astype(BF16)
    cim = (-pack_c(c_im)).astype(BF16)
    ar = a_re.reshape(1, N_STATE)
    ai = a_im.reshape(1, N_STATE)
    pw = [(ar, ai)]
    for _ in range(SUBLANES - 1):
        pr, pi = pw[-1]
        pw.append((pr * ar - pi * ai, pr * ai + pi * ar))
    row = jnp.arange(SUBLANES)[:, None]

    def masked(k):
        return (jnp.where(row >= k, pw[k - 1][0], 0.0), jnp.where(row >= k, pw[k - 1][1], 0.0))

    m1, m2, m4 = masked(1), masked(2), masked(4)
    a_r = jnp.concatenate([p[0] for p in pw], axis=0)
    a_i = jnp.concatenate([p[1] for p in pw], axis=0)
    tabs = jnp.stack([m1[0], m1[1], m2[0], m2[1], m4[0], m4[1], a_r, a_i])
    a1 = jnp.concatenate([ar, ai], axis=0)
    return bblk, cre, cim, tabs, a1


def _s5_out(x, u, y, d_ref, wa_ref, wb_ref):
    y = y + d_ref[...] * u
    hb = jax.nn.gelu(y).astype(BF16)
    return x + _dot(hb, wa_ref[...]) * jax.nn.sigmoid(_dot(hb, wb_ref[...]))


def _s5_prompt_kernel(x_ref, g_ref, bblk_ref, cre_ref, cim_ref, tabs_ref, d_ref, wa_ref, wb_ref,
                      h_ref, sre_ref, sim_ref, cr_s, ci_s, xr_s, xi_s):
    @pl.when(pl.program_id(0) == 0)
    def _():
        cr_s[...] = jnp.zeros_like(cr_s)
        ci_s[...] = jnp.zeros_like(ci_s)

    x = x_ref[...]
    u = _rmsn(x) * g_ref[...]
    ub = u.astype(BF16)
    n8 = x.shape[0] // SUBLANES
    ys = []
    for j in range(S5_BLOCKS):
        lo, hi = j * S5_ST, (j + 1) * S5_ST
        bu = _dot(ub[:, j * S5_IN:(j + 1) * S5_IN], bblk_ref[j])
        xr = bu[:, :S5_ST].reshape(n8, SUBLANES, S5_ST)
        xi = bu[:, S5_ST:].reshape(n8, SUBLANES, S5_ST)
        for k, sh in ((0, 1), (2, 2), (4, 4)):
            mr = tabs_ref[k, :, lo:hi]
            mi = tabs_ref[k + 1, :, lo:hi]
            rr = pltpu.roll(xr, sh, 1)
            ri = pltpu.roll(xi, sh, 1)
            xr, xi = xr + (mr * rr - mi * ri), xi + (mr * ri + mi * rr)
        a_r = tabs_ref[6, :, lo:hi]
        a_i = tabs_ref[7, :, lo:hi]
        pr = cr_s[:, lo:hi]
        pi = ci_s[:, lo:hi]
        for t in range(n8):
            br = jnp.broadcast_to(pr, (SUBLANES, S5_ST))
            bi = jnp.broadcast_to(pi, (SUBLANES, S5_ST))
            tr = xr[t] + (a_r * br - a_i * bi)
            ti = xi[t] + (a_r * bi + a_i * br)
            xr_s[t * SUBLANES:(t + 1) * SUBLANES, :] = tr
            xi_s[t * SUBLANES:(t + 1) * SUBLANES, :] = ti
            pr = tr[SUBLANES - 1:SUBLANES, :]
            pi = ti[SUBLANES - 1:SUBLANES, :]
        cr_s[:, lo:hi] = pr
        ci_s[:, lo:hi] = pi
        ys.append(_dot(xr_s[...].astype(BF16), cre_ref[j]) + _dot(xi_s[...].astype(BF16), cim_ref[j]))
    y = jnp.concatenate(ys, axis=1)
    h_ref[...] = _s5_out(x, u, y, d_ref, wa_ref, wb_ref)
    sre_ref[...] = cr_s[...]
    sim_ref[...] = ci_s[...]


def _s5_prompt(x, g, bblk, cre, cim, tabs, d, wa, wb, chunk):
    t_len = x.shape[0]
    chunk = min(chunk, t_len)
    return pl.pallas_call(
        _s5_prompt_kernel,
        grid=(t_len // chunk,),
        in_specs=[pl.BlockSpec((chunk, D_MODEL), lambda i: (i, 0)),
                  _const_spec(g.shape), _const_spec(bblk.shape), _const_spec(cre.shape), _const_spec(cim.shape),
                  _const_spec(tabs.shape), _const_spec(d.shape), _const_spec(wa.shape), _const_spec(wb.shape)],
        out_specs=[pl.BlockSpec((chunk, D_MODEL), lambda i: (i, 0)),
                   pl.BlockSpec((1, N_STATE), lambda i: (0, 0)),
                   pl.BlockSpec((1, N_STATE), lambda i: (0, 0))],
        out_shape=[jax.ShapeDtypeStruct((t_len, D_MODEL), F32),
                   jax.ShapeDtypeStruct((1, N_STATE), F32),
                   jax.ShapeDtypeStruct((1, N_STATE), F32)],
        scratch_shapes=[pltpu.VMEM((1, N_STATE), F32), pltpu.VMEM((1, N_STATE), F32),
                        pltpu.VMEM((chunk, S5_ST), F32), pltpu.VMEM((chunk, S5_ST), F32)],
        compiler_params=_params(("arbitrary",)),
        name="s5_prompt",
    )(x, g, bblk, cre, cim, tabs, d, wa, wb)


def _s5_sample_kernel(x_ref, g_ref, bblk_ref, cre_ref, cim_ref, a_ref, s0r_ref, s0i_ref, d_ref, wa_ref, wb_ref,
                      h_ref, sre_ref, sim_ref, xr_s, xi_s, *, n_seq, n_tok):
    x = x_ref[...]
    u = _rmsn(x) * g_ref[...]
    ub = u.astype(BF16)
    ys = []
    for j in range(S5_BLOCKS):
        lo, hi = j * S5_ST, (j + 1) * S5_ST
        bu = _dot(ub[:, j * S5_IN:(j + 1) * S5_IN], bblk_ref[j])
        ar = a_ref[0:1, lo:hi]
        ai = a_ref[1:2, lo:hi]
        sr = s0r_ref[:, lo:hi]
        si = s0i_ref[:, lo:hi]
        for t in range(n_tok):
            rows = bu[t * n_seq:(t + 1) * n_seq]
            sr, si = ar * sr - ai * si + rows[:, :S5_ST], ar * si + ai * sr + rows[:, S5_ST:]
            xr_s[t * n_seq:(t + 1) * n_seq, :] = sr
            xi_s[t * n_seq:(t + 1) * n_seq, :] = si
        sre_ref[:, lo:hi] = sr
        sim_ref[:, lo:hi] = si
        ys.append(_dot(xr_s[...].astype(BF16), cre_ref[j]) + _dot(xi_s[...].astype(BF16), cim_ref[j]))
    y = jnp.concatenate(ys, axis=1)
    h_ref[...] = _s5_out(x, u, y, d_ref, wa_ref, wb_ref)


def _s5_sample(x_tm, g, bblk, cre, cim, a1, s0r, s0i, d, wa, wb, n_seq, n_tok):
    n = n_seq * n_tok
    args = (x_tm, g, bblk, cre, cim, a1, s0r, s0i, d, wa, wb)
    return pl.pallas_call(
        functools.partial(_s5_sample_kernel, n_seq=n_seq, n_tok=n_tok),
        grid=(1,),
        in_specs=[_const_spec(a.shape) for a in args],
        out_specs=[_const_spec((n, D_MODEL)), _const_spec((n_seq, N_STATE)), _const_spec((n_seq, N_STATE))],
        out_shape=[jax.ShapeDtypeStruct((n, D_MODEL), F32),
                   jax.ShapeDtypeStruct((n_seq, N_STATE), F32),
                   jax.ShapeDtypeStruct((n_seq, N_STATE), F32)],
        scratch_shapes=[pltpu.VMEM((n, S5_ST), F32), pltpu.VMEM((n, S5_ST), F32)],
        compiler_params=_params(("arbitrary",)),
        name="s5_sample",
    )(*args)


def _ffn0_kernel(h1_ref, p_ref, gf_ref, w1_ref, w3_ref, w2_ref, wple_ref, wgate_ref, gkv_ref, gq_ref,
                 wk_ref, wv_ref, wq_ref, h3_ref, k_ref, v_ref, q_ref, *km_ref, n_blk):
    x = h1_ref[...]
    hn = (_rmsn(x) * gf_ref[...]).astype(BF16)
    h2 = x
    for c in range(2):
        lo, hi = c * FF_HALF, (c + 1) * FF_HALF
        a = _dot(hn, w1_ref[:, lo:hi])
        b = _dot(hn, w3_ref[:, lo:hi])
        h2 = h2 + _dot((jax.nn.silu(a) * b).astype(BF16), w2_ref[lo:hi, :])
    ple = _dot(p_ref[...].astype(BF16), wple_ref[...]) * jax.nn.sigmoid(_dot(h2.astype(BF16), wgate_ref[...]))
    h3 = h2 + ple
    h3_ref[...] = h3
    xn = _rmsn(h3)
    kvb = (xn * gkv_ref[...]).astype(BF16)
    qb = (xn * gq_ref[...]).astype(BF16)
    k = _dot(kvb, wk_ref[...])
    k_ref[...] = k
    v_ref[...] = _dot(kvb, wv_ref[...])
    q_ref[...] = _dot(qb, wq_ref[...])
    for b in range(n_blk):
        km_ref[0][b] = jnp.mean(k[b * MOBA_BLOCK:(b + 1) * MOBA_BLOCK], axis=0, keepdims=True)


def _ffn0(h1, p, gf, w1, w3, w2, wple, wgate, gkv, gq, wk, wv, wq, tm, with_kmean):
    n = h1.shape[0]
    tm = min(tm, n)
    n_blk = tm // MOBA_BLOCK if with_kmean else 0
    consts = (gf, w1, w3, w2, wple, wgate, gkv, gq, wk, wv, wq)
    tile = pl.BlockSpec((tm, D_MODEL), lambda i: (i, 0))
    out_specs = [tile, tile, tile, tile]
    out_shape = [jax.ShapeDtypeStruct((n, D_MODEL), F32)] * 4
    if with_kmean:
        out_specs.append(pl.BlockSpec((n_blk, 1, D_MODEL), lambda i: (i, 0, 0)))
        out_shape.append(jax.ShapeDtypeStruct((n // MOBA_BLOCK, 1, D_MODEL), F32))
    return pl.pallas_call(
        functools.partial(_ffn0_kernel, n_blk=n_blk),
        grid=(n // tm,),
        in_specs=[tile, pl.BlockSpec((tm, PLE_DIM), lambda i: (i, 0))] + [_const_spec(c.shape) for c in consts],
        out_specs=out_specs,
        out_shape=out_shape,
        compiler_params=_params(("arbitrary",)),
        name="ffn0_qkv",
    )(h1, p, *consts)


def _rel_bucket(dist):
    max_exact = REL_BUCKETS // 2
    d = jnp.maximum(dist, 0)
    large = max_exact + (jnp.log(jnp.maximum(d, max_exact).astype(F32) / max_exact)
                         / math.log(REL_MAX_DIST / max_exact) * (REL_BUCKETS - max_exact)).astype(jnp.int32)
    large = jnp.minimum(large, REL_BUCKETS - 1)
    return jnp.where(d < max_exact, d, large)


def _bias_by_distance(rel_bias, n):
    return rel_bias[_rel_bucket(jnp.arange(n))]


FAR_DIST = REL_MAX_DIST


def _top3_rows(cur, rows, n_rows):
    picked = jnp.zeros(cur.shape, jnp.bool_)
    for _ in range(MOBA_TOPK):
        m = jnp.max(cur, axis=0, keepdims=True)
        idx = jnp.min(jnp.where(cur == m, rows, n_rows), axis=0, keepdims=True)
        hit = rows == idx
        picked = jnp.logical_or(picked, hit)
        cur = jnp.where(hit, -jnp.inf, cur)
    return picked


def _moba_prompt_kernel(q_ref, k_ref, v_ref, km_ref, bias_ref, o_ref, sel_s, s_s, smax_s, *, n_blk):
    own = pl.program_id(1)
    q = q_ref[0]
    tq = q.shape[0]
    q_hi, q_lo = _split2(q)
    km_hi, km_lo = _split2(km_ref[0])
    sc = _dot_nt(km_hi, q_hi) + (_dot_nt(km_hi, q_lo) + _dot_nt(km_lo, q_hi))
    rows = lax.broadcasted_iota(jnp.int32, sc.shape, 0)
    past = rows < own
    picked = _top3_rows(jnp.where(past, sc, NEG), rows, n_blk)
    sel = jnp.where(jnp.logical_and(picked, past), 1.0, 0.0)
    for n in range(n_blk):
        sel_s[n] = sel[n:n + 1, :]

    qb = (q * (HEAD_DIM ** -0.5 * LOG2E)).astype(BF16)
    grp = FAR_GROUP

    def softmax_step(carry, s3, smax, on3, vs):
        m_run, acc = carry
        m_new = jnp.maximum(m_run, jnp.max(jnp.where(on3, smax, NEG), axis=0))
        alpha = jnp.exp2(m_run - m_new)
        pb = jnp.exp2(s3 - jnp.where(on3, m_new[None], -NEG)).astype(BF16)
        acc = alpha * acc
        for j, vj in enumerate(vs):
            acc = acc + _dot(vj, pb[j])
        return m_new, acc

    n_far = jnp.maximum(own - 1, 0)
    blk_iota = lax.broadcasted_iota(jnp.int32, (grp, 1, tq), 0)

    def produce(g, slot):
        start = jnp.minimum(g * grp, n_blk - grp)
        for j in range(grp):
            s = _dot_nt(k_ref[0, start + j], qb)
            s_s[slot, j] = s
            smax_s[slot, j] = jnp.max(s, axis=0, keepdims=True)

    def consume(g, slot, carry):
        start = jnp.minimum(g * grp, n_blk - grp)
        on3 = jnp.logical_and(sel_s[pl.ds(start, grp)] > 0.0, blk_iota + g * grp < n_far)
        return softmax_step(carry, s_s[slot], smax_s[slot], on3, [v_ref[0, start + j] for j in range(grp)])

    def far_body(i, carry):
        produce(2 * i + 1, 1)
        carry = consume(2 * i, 0, carry)
        produce(2 * i + 2, 0)
        return consume(2 * i + 1, 1, carry)

    init = (jnp.full((1, tq), NEG, F32), jnp.zeros((HEAD_DIM + DEN_ROWS, tq), F32))
    produce(0, 0)
    n_grp = (n_far + grp - 1) // grp
    carry = lax.fori_loop(0, (n_grp + 1) // 2, far_body, init)
    prev = jnp.maximum(own - 1, 0)
    k2 = jnp.concatenate([k_ref[0, prev], k_ref[0, own]], axis=0)
    s2 = _dot_nt(k2, qb).reshape(2, MOBA_BLOCK, tq) + bias_ref[0]
    on2 = jnp.concatenate([sel_s[pl.ds(prev, 1)] > 0.0, jnp.ones((1, 1, tq), jnp.bool_)], axis=0)
    _, acc = softmax_step(carry, s2, jnp.max(s2, axis=1, keepdims=True), on2, [v_ref[0, prev], v_ref[0, own]])
    o_ref[0] = acc[:HEAD_DIM] / acc[HEAD_DIM:HEAD_DIM + 1]


def _moba_prompt(q, k, v, kmean, rel_bias):
    t_len = q.shape[0]
    n_blk = t_len // MOBA_BLOCK
    qh = q.reshape(t_len, N_HEADS, HEAD_DIM).transpose(1, 0, 2)
    kh = k.astype(BF16).reshape(n_blk, MOBA_BLOCK, N_HEADS, HEAD_DIM).transpose(2, 0, 1, 3)
    vt = v.astype(BF16).reshape(n_blk, MOBA_BLOCK, N_HEADS, HEAD_DIM).transpose(2, 0, 3, 1)
    vt = jnp.concatenate([vt, jnp.ones((N_HEADS, n_blk, DEN_ROWS, MOBA_BLOCK), BF16)], axis=2)
    kmh = kmean.reshape(n_blk, N_HEADS, HEAD_DIM).transpose(1, 0, 2)
    nb2 = 2 * MOBA_BLOCK
    bd = ((_bias_by_distance(rel_bias, nb2) - rel_bias[REL_BUCKETS - 1]) * LOG2E).T

    def toeplitz(a):
        flat = jnp.tile(a, (1, MOBA_BLOCK))[:, :MOBA_BLOCK * (nb2 - 1)]
        return flat.reshape(N_HEADS, MOBA_BLOCK, nb2 - 1)[:, :, :MOBA_BLOCK]

    own_t = toeplitz(jnp.concatenate([bd[:, :MOBA_BLOCK], jnp.full((N_HEADS, MOBA_BLOCK), NEG, F32)], axis=1))
    prev_t = toeplitz(jnp.concatenate([bd[:, MOBA_BLOCK:], bd[:, :MOBA_BLOCK]], axis=1))
    bias = jnp.stack([prev_t, own_t], axis=1)
    assert n_blk % FAR_GROUP == 0
    out = pl.pallas_call(
        functools.partial(_moba_prompt_kernel, n_blk=n_blk),
        grid=(N_HEADS, n_blk),
        in_specs=[pl.BlockSpec((1, MOBA_BLOCK, HEAD_DIM), lambda h, i: (h, i, 0)),
                  pl.BlockSpec((1, n_blk, MOBA_BLOCK, HEAD_DIM), lambda h, i: (h, 0, 0, 0)),
                  pl.BlockSpec((1, n_blk, HEAD_DIM + DEN_ROWS, MOBA_BLOCK), lambda h, i: (h, 0, 0, 0)),
                  pl.BlockSpec((1, n_blk, HEAD_DIM), lambda h, i: (h, 0, 0)),
                  pl.BlockSpec((1, 2, MOBA_BLOCK, MOBA_BLOCK), lambda h, i: (h, 0, 0, 0))],
        out_specs=pl.BlockSpec((1, HEAD_DIM, MOBA_BLOCK), lambda h, i: (h, 0, i)),
        out_shape=jax.ShapeDtypeStruct((N_HEADS, HEAD_DIM, t_len), F32),
        scratch_shapes=[pltpu.VMEM((n_blk, 1, MOBA_BLOCK), F32),
                        pltpu.VMEM((2, FAR_GROUP, MOBA_BLOCK, MOBA_BLOCK), F32),
                        pltpu.VMEM((2, FAR_GROUP, 1, MOBA_BLOCK), F32)],
        compiler_params=_params(("arbitrary", "arbitrary")),
        name="moba_prompt",
    )(qh, kh, vt, kmh, bias)
    return out.transpose(2, 0, 1).reshape(t_len, D_MODEL)


def _page_mean_kernel(pt_ref, *refs):
    del pt_ref
    ones_ref, o_ref = refs[-2], refs[-1]
    for b in range(PAGES_PER_STEP // PAGES_PER_BLOCK):
        x = (refs[2 * b][0] + refs[2 * b + 1][0]).reshape(D_MODEL, PAGE_SIZE)
        hi, lo = _split2(x)
        r = _dot_nt(ones_ref[...], hi) + _dot_nt(ones_ref[...], lo)
        o_ref[b:b + 1, :] = r[0:1] * (1.0 / MOBA_BLOCK)


def _block_means(cache_kt, pt_flat):
    n_pages_total = pt_flat.shape[0]
    steps = n_pages_total // PAGES_PER_STEP
    blocks_per_step = PAGES_PER_STEP // PAGES_PER_BLOCK
    ones = jnp.ones((SUBLANES, PAGE_SIZE), BF16)

    def page_spec(r):
        return pl.BlockSpec((1, N_HEADS, HEAD_DIM, PAGE_SIZE),
                            lambda i, pt: (pt[i * PAGES_PER_STEP + r], 0, 0, 0))

    return pl.pallas_call(
        _page_mean_kernel,
        grid_spec=pltpu.PrefetchScalarGridSpec(
            num_scalar_prefetch=1, grid=(steps,),
            in_specs=[page_spec(r) for r in range(PAGES_PER_STEP)] + [_const_spec(ones.shape)],
            out_specs=pl.BlockSpec((blocks_per_step, D_MODEL), lambda i, pt: (i, 0))),
        out_shape=jax.ShapeDtypeStruct((n_pages_total // PAGES_PER_BLOCK, D_MODEL), F32),
        compiler_params=_params(("arbitrary",)),
        name="page_means",
    )(pt_flat, *([cache_kt] * PAGES_PER_STEP), ones)


def _sample_select_kernel(q_ref, pm_ref, ones_ref, idx_ref, *, n_tok, n_past):
    pm = pm_ref[...]
    blk = lax.broadcasted_iota(jnp.int32, (n_past, N_HEADS, LANES), 0)
    for t in range(n_tok):
        prod = (pm * q_ref[0, t]).reshape(n_past * N_HEADS, HEAD_DIM)
        p1 = prod.astype(BF16)
        r1 = prod - p1.astype(F32)
        p2 = r1.astype(BF16)
        p3 = (r1 - p2.astype(F32)).astype(BF16)
        sc = _dot(p1, ones_ref[...]) + (_dot(p2, ones_ref[...]) + _dot(p3, ones_ref[...]))
        cur = sc.reshape(n_past, N_HEADS, LANES)
        for i in range(MOBA_TOPK):
            m = jnp.max(cur, axis=0, keepdims=True)
            idx = jnp.min(jnp.where(cur == m, blk, n_past), axis=0)
            idx_ref[0, t, i] = idx
            cur = jnp.where(blk == idx[None], -jnp.inf, cur)


def _sample_select(q4, pm, n_past):
    n_seq, n_tok = q4.shape[:2]
    ones = jnp.ones((HEAD_DIM, LANES), BF16)
    return pl.pallas_call(
        functools.partial(_sample_select_kernel, n_tok=n_tok, n_past=n_past),
        grid=(n_seq,),
        in_specs=[pl.BlockSpec((1, n_tok, N_HEADS, HEAD_DIM), lambda b: (b, 0, 0, 0)),
                  pl.BlockSpec((n_past, N_HEADS, HEAD_DIM), lambda b: (b, 0, 0)),
                  _const_spec(ones.shape)],
        out_specs=pl.BlockSpec((1, n_tok, MOBA_TOPK, N_HEADS, LANES), lambda b: (b, 0, 0, 0, 0)),
        out_shape=jax.ShapeDtypeStruct((n_seq, n_tok, MOBA_TOPK, N_HEADS, LANES), jnp.int32),
        compiler_params=_params(("arbitrary",)),
        name="sample_select",
    )(q4, pm, ones)


N_SEL_PAGES = MOBA_TOPK * PAGES_PER_BLOCK


def _sample_attn_kernel(sel_ref, pt_ref, ck_ref, cv_ref, q_ref, kn_ref, vn_ref, near_ref, far_ref, ownb_ref,
                        o_ref, kbuf, vbuf, sems, *, n_tok, n_past, n_pages, n_steps):
    i = pl.program_id(0)
    t = i % n_tok
    slot = i % 2

    def step_copies(step, dst_slot):
        seq = step // n_tok
        cps = []
        for h in range(N_HEADS):
            base = (step * N_HEADS + h) * MOBA_TOPK
            for s in range(N_SEL_PAGES):
                blk = sel_ref[base + s // PAGES_PER_BLOCK]
                page = pt_ref[seq * n_pages + blk * PAGES_PER_BLOCK + s % PAGES_PER_BLOCK]
                lanes = pl.ds(s * PAGE_SIZE, PAGE_SIZE)
                cps.append(pltpu.make_async_copy(ck_ref.at[page, h], kbuf.at[dst_slot, h, :, lanes],
                                                 sems.at[0, dst_slot]))
                cps.append(pltpu.make_async_copy(cv_ref.at[page, h], vbuf.at[dst_slot, h, :, lanes],
                                                 sems.at[1, dst_slot]))
        return cps

    @pl.when(i == 0)
    def _():
        for cp in step_copies(0, 0):
            cp.start()

    @pl.when(i + 1 < n_steps)
    def _():
        for cp in step_copies(i + 1, 1 - slot):
            cp.start()

    for cp in step_copies(i, slot):
        cp.wait()

    qss, lgs = [], []
    for h in range(N_HEADS):
        qs = q_ref[0, 0, h:h + 1, :] * (HEAD_DIM ** -0.5)
        q8 = jnp.broadcast_to(qs, (SUBLANES, HEAD_DIM)).astype(BF16)
        qss.append(qs)
        lgs.append(_dot(q8, kbuf[slot, h].astype(BF16))[0:1])
    p8s, p_owns, dens = [], [], []
    for h in range(N_HEADS):
        base = (i * N_HEADS + h) * MOBA_TOPK
        biases = []
        for s in range(N_SEL_PAGES):
            blk = sel_ref[base + s // PAGES_PER_BLOCK]
            is_near = jnp.logical_and(blk == n_past - 1, (s % PAGES_PER_BLOCK) == PAGES_PER_BLOCK - 1)
            biases.append(jnp.where(is_near, near_ref[t * N_HEADS + h], far_ref[h]))
        lg = lgs[h] + jnp.concatenate(biases, axis=1)
        l_own = jnp.sum(kn_ref[0, :, h, :] * qss[h], axis=1, keepdims=True) + ownb_ref[t * N_HEADS + h]
        m = jnp.maximum(jnp.max(l_own, axis=0, keepdims=True), jnp.max(lg, axis=1, keepdims=True))
        p_own = jnp.exp(l_own - m)
        p = jnp.exp(lg - m)
        dens.append(jnp.sum(p_own, axis=0, keepdims=True) + jnp.sum(p, axis=1, keepdims=True))
        p_owns.append(p_own)
        p8s.append(jnp.broadcast_to(p, (SUBLANES, N_SEL_PAGES * PAGE_SIZE)).astype(BF16))
    for h in range(N_HEADS):
        o = (jnp.sum(p_owns[h] * vn_ref[0, :, h, :], axis=0, keepdims=True)
             + _dot_nt(p8s[h], vbuf[slot, h].astype(BF16))[0:1])
        o_ref[0, 0, h:h + 1, :] = o / dens[h]


def _moba_sample(q, k_new, v_new, cache_k, cache_v, page_table, rel_bias):
    n_seq, n_pages = page_table.shape
    n_tok = q.shape[0] // n_seq
    past_len = n_pages * PAGE_SIZE
    assert past_len % MOBA_BLOCK == 0, "own block must start at the first new token"
    n_past = past_len // MOBA_BLOCK
    assert n_past >= MOBA_TOPK and (n_seq * n_pages) % PAGES_PER_STEP == 0
    pt_flat = page_table.reshape(-1)
    cache_kt = cache_k.transpose(0, 2, 3, 1)
    cache_vt = cache_v.transpose(0, 2, 3, 1)
    pm = _block_means(cache_kt, pt_flat).reshape(-1, N_HEADS, HEAD_DIM)
    q4 = q.reshape(n_seq, n_tok, N_HEADS, HEAD_DIM)
    idx = _sample_select(q4, pm, n_past)
    sel = idx[..., 0].transpose(0, 1, 3, 2).reshape(-1)
    assert PAGE_SIZE + 1 >= FAR_DIST
    bd = _bias_by_distance(rel_bias, PAGE_SIZE + n_tok)
    tt = jnp.arange(n_tok)
    near = bd[PAGE_SIZE + tt[:, None] - jnp.arange(PAGE_SIZE)[None, :]]
    near = near.transpose(0, 2, 1).reshape(n_tok * N_HEADS, 1, PAGE_SIZE)
    far = jnp.broadcast_to(rel_bias[REL_BUCKETS - 1][:, None, None], (N_HEADS, 1, PAGE_SIZE))
    dd = tt[:, None] - tt[None, :]
    ownb = jnp.where((dd >= 0)[..., None], bd[jnp.maximum(dd, 0)], NEG)
    ownb = ownb.transpose(0, 2, 1).reshape(n_tok * N_HEADS, n_tok, 1)
    kn4 = k_new.reshape(n_seq, n_tok, N_HEADS, HEAD_DIM)
    vn4 = v_new.reshape(n_seq, n_tok, N_HEADS, HEAD_DIM)
    n_steps = n_seq * n_tok
    tok_spec = pl.BlockSpec((1, 1, N_HEADS, HEAD_DIM), lambda i, *_: (i // n_tok, i % n_tok, 0, 0))
    new_spec = pl.BlockSpec((1, n_tok, N_HEADS, HEAD_DIM), lambda i, *_: (i // n_tok, 0, 0, 0))
    gathered = pltpu.VMEM((2, N_HEADS, HEAD_DIM, N_SEL_PAGES * PAGE_SIZE), F32)
    out = pl.pallas_call(
        functools.partial(_sample_attn_kernel, n_tok=n_tok, n_past=n_past, n_pages=n_pages, n_steps=n_steps),
        grid_spec=pltpu.PrefetchScalarGridSpec(
            num_scalar_prefetch=2, grid=(n_steps,),
            in_specs=[pl.BlockSpec(memory_space=pl.ANY), pl.BlockSpec(memory_space=pl.ANY),
                      tok_spec, new_spec, new_spec,
                      _const_spec(near.shape), _const_spec(far.shape), _const_spec(ownb.shape)],
            out_specs=tok_spec,
            scratch_shapes=[gathered, gathered, pltpu.SemaphoreType.DMA((2, 2))]),
        out_shape=jax.ShapeDtypeStruct((n_seq, n_tok, N_HEADS, HEAD_DIM), F32),
        compiler_params=_params(("arbitrary",)),
        name="moba_sample",
    )(sel, pt_flat, cache_kt, cache_vt, q4, kn4, vn4, near, far, ownb)
    return out.reshape(n_seq * n_tok, D_MODEL)


def _moe_kernel(h3_ref, at_ref, p_ref, wo_ref, gf_ref, rtt_ref, utri_ref, w1_ref, w3_ref, w2_ref, wple_ref,
                wgate_ref, gfin_ref, y_ref, hn_s, grow_s, rrow_s, rcol_s, cnt_s, x_s, yc_s, *, rows):
    e = pl.program_id(1)
    half = pl.program_id(2)
    tc = hn_s.shape[0]

    slab = min(tc, MOE_SLAB)
    slabs = [slice(s, s + slab) for s in range(0, tc, slab)]

    def stream_in(sl):
        return h3_ref[sl, :] + _dot(at_ref[sl, :].astype(BF16), wo_ref[...])

    @pl.when(jnp.logical_and(e == 0, half == 0))
    def _():
        rt_hi, rt_lo = _split2(rtt_ref[...])
        lts = []
        for sl in slabs:
            hn = _rmsn(stream_in(sl)) * gf_ref[...]
            hn_s[sl, :] = hn.astype(BF16)
            hn_hi, hn_lo = _split2(hn)
            lts.append(_dot_nt(rt_hi, hn_hi) + (_dot_nt(rt_hi, hn_lo) + _dot_nt(rt_lo, hn_hi)))
            y_ref[sl, :] = jnp.zeros((slab, D_MODEL), F32)
        lt = jnp.concatenate(lts, axis=1)
        ex = jnp.exp(lt - jnp.max(lt, axis=0, keepdims=True))
        probs = ex / jnp.sum(ex, axis=0, keepdims=True)
        row = lax.broadcasted_iota(jnp.int32, probs.shape, 0)
        p1 = jnp.max(probs, axis=0, keepdims=True)
        i1 = jnp.min(jnp.where(probs == p1, row, N_EXPERTS), axis=0, keepdims=True)
        rest = jnp.where(row == i1, -1.0, probs)
        p2 = jnp.max(rest, axis=0, keepdims=True)
        i2 = jnp.min(jnp.where(rest == p2, row, N_EXPERTS), axis=0, keepdims=True)
        tot = p1 + p2
        gates = jnp.where(row == i1, p1 / tot, 0.0) + jnp.where(row == i2, p2 / tot, 0.0)
        mem = jnp.where(jnp.logical_or(row == i1, row == i2), 1.0, 0.0)
        rank = jnp.where(mem > 0.0, _dot(mem.astype(BF16), utri_ref[...]), -1.0)
        for k in range(N_EXPERTS):
            grow_s[k] = gates[k:k + 1]
            rrow_s[k] = rank[k:k + 1]
            cnt_s[k] = jnp.sum(mem[k:k + 1]).astype(jnp.int32)
        rcol_s[...] = rank.T

    n_tiles = (cnt_s[e] + rows - 1) // rows

    def one_hot_rows(j):
        r_iota = lax.broadcasted_iota(jnp.int32, (rows, tc), 0) + j * rows
        return jnp.where(rrow_s[e] == r_iota.astype(F32), 1.0, 0.0)

    @pl.when(half == 0)
    def _():
        def compact(j, _):
            x_s[j] = _dot(one_hot_rows(j).astype(BF16), hn_s[...]).astype(BF16)
            return 0
        lax.fori_loop(0, n_tiles, compact, 0)

    def expert_tile(j, _):
        x = x_s[j]
        a = _dot(x, w1_ref[0])
        b = _dot(x, w3_ref[0])
        y = _dot((jax.nn.silu(a) * b).astype(BF16), w2_ref[0])

        @pl.when(half == 0)
        def _():
            yc_s[j] = y

        @pl.when(half == 1)
        def _():
            g_rows = jnp.sum(one_hot_rows(j) * grow_s[e], axis=1, keepdims=True)
            ys = g_rows * (yc_s[j] + y)
            cols = -(-rows // LANES) * LANES
            if cols > rows:
                ys = jnp.concatenate([ys, jnp.zeros((cols - rows, D_MODEL), F32)], axis=0)
            hi, lo = _split2(ys)
            lane_e = lax.broadcasted_iota(jnp.int32, rcol_s.shape, 1)
            rank_col = jnp.sum(jnp.where(lane_e == e, rcol_s[...], 0.0), axis=1, keepdims=True)
            c_iota = lax.broadcasted_iota(jnp.int32, (tc, cols), 1) + j * rows
            scat = jnp.where(rank_col == c_iota.astype(F32), 1.0, 0.0).astype(BF16)
            for sl in slabs:
                y_ref[sl, :] += _dot(scat[sl, :], hi) + _dot(scat[sl, :], lo)
        return 0

    lax.fori_loop(0, n_tiles, expert_tile, 0)

    @pl.when(jnp.logical_and(e == N_EXPERTS - 1, half == 1))
    def _():
        for sl in slabs:
            h5 = stream_in(sl) + y_ref[sl, :]
            ple = (_dot(p_ref[sl, :].astype(BF16), wple_ref[...])
                   * jax.nn.sigmoid(_dot(h5.astype(BF16), wgate_ref[...])))
            y_ref[sl, :] = _rmsn(h5 + ple) * gfin_ref[...]


def _moe_layer(h3, attn, p, wo, gf, rtt, w1, w3, w2, wple, wgate, gfin, chunk, rows):
    n = h3.shape[0]
    tc = min(chunk, n)
    rows = min(rows, tc)
    max_tiles = -(-tc // rows)
    utri = jnp.triu(jnp.ones((tc, tc), BF16), k=1)
    tile = pl.BlockSpec((tc, D_MODEL), lambda i, e, c: (i, 0), pipeline_mode=pl.Buffered(1))
    return pl.pallas_call(
        functools.partial(_moe_kernel, rows=rows),
        grid=(n // tc, N_EXPERTS, 2),
        in_specs=[tile, tile, pl.BlockSpec((tc, PLE_DIM), lambda i, e, c: (i, 0), pipeline_mode=pl.Buffered(1)),
                  _const_spec(wo.shape), _const_spec(gf.shape), _const_spec(rtt.shape), _const_spec(utri.shape),
                  pl.BlockSpec((1, D_MODEL, FF_HALF), lambda i, e, c: (e, 0, c)),
                  pl.BlockSpec((1, D_MODEL, FF_HALF), lambda i, e, c: (e, 0, c)),
                  pl.BlockSpec((1, FF_HALF, D_MODEL), lambda i, e, c: (e, c, 0)),
                  _const_spec(wple.shape), _const_spec(wgate.shape), _const_spec(gfin.shape)],
        out_specs=pl.BlockSpec((tc, D_MODEL), lambda i, e, c: (i, 0)),
        out_shape=jax.ShapeDtypeStruct((n, D_MODEL), F32),
        scratch_shapes=[pltpu.VMEM((tc, D_MODEL), BF16),
                        pltpu.VMEM((N_EXPERTS, 1, tc), F32), pltpu.VMEM((N_EXPERTS, 1, tc), F32),
                        pltpu.VMEM((tc, N_EXPERTS), F32), pltpu.SMEM((N_EXPERTS,), jnp.int32),
                        pltpu.VMEM((max_tiles, rows, D_MODEL), BF16), pltpu.VMEM((max_tiles, rows, D_MODEL), F32)],
        compiler_params=_params(("arbitrary", "arbitrary", "arbitrary"), vmem=MOE_VMEM_LIMIT),
        name="oproj_moe",
    )(h3, attn, p, wo, gf, rtt, utri, w1, w3, w2, wple, wgate, gfin)


S5_CHUNK = 256
TOKEN_TILE = 512
MOE_CHUNK = 1024
MOE_ROWS = 320
MOE_SLAB = 256


def kernel(x_prompt, x_sample, cache_k, cache_v, state_ssm_re, state_ssm_im, page_table, p_prompt, p_sample,
           rel_bias, norm_mix, norm_ffn, norm_kv, norm_final, ssm_lam_re, ssm_lam_im, ssm_log_dt, ssm_b_re,
           ssm_b_im, ssm_c_re, ssm_c_im, ssm_d, w_glu_a, w_glu_b, w_q, w_k, w_v, w_o, ffn_w1, ffn_w3, ffn_w2,
           router, moe_w1, moe_w3, moe_w2, w_ple, w_ple_gate):
    assert x_prompt.shape[0] == 1 and norm_mix.shape[0] == 2, "one prompt sequence, depth 2"
    t_len = x_prompt.shape[1]
    n_seq, n_tok, _ = x_sample.shape
    row = lambda v: v.reshape(1, -1).astype(F32)
    bf = lambda w: w.astype(BF16)

    bblk, cre, cim, tabs, a1 = _s5_tables(ssm_lam_re[0], ssm_lam_im[0], ssm_log_dt[0], ssm_b_re[0], ssm_b_im[0],
                                          ssm_c_re[0], ssm_c_im[0])
    g_mix0, g_mix1 = row(norm_mix[0]), row(norm_mix[1])
    g_ffn0, g_ffn1 = row(norm_ffn[0]), row(norm_ffn[1])
    g_kv, g_fin, d_skip = row(norm_kv), row(norm_final), row(ssm_d[0])
    wa, wb = bf(w_glu_a[0]), bf(w_glu_b[0])
    f_w1, f_w3, f_w2 = bf(ffn_w1[0]), bf(ffn_w3[0]), bf(ffn_w2[0])
    wple0, wple1 = bf(w_ple[0]), bf(w_ple[1])
    wgate0, wgate1 = bf(w_ple_gate[0]), bf(w_ple_gate[1])
    wk, wv, wq, wo = bf(w_k), bf(w_v), bf(w_q[0]), bf(w_o[0])
    m_w1, m_w3, m_w2 = bf(moe_w1[0]), bf(moe_w3[0]), bf(moe_w2[0])
    rtt = router[0].astype(F32).T

    def ffn0(h1, p, with_kmean):
        return _ffn0(h1, p, g_ffn0, f_w1, f_w3, f_w2, wple0, wgate0, g_kv, g_mix1, wk, wv, wq,
                     TOKEN_TILE, with_kmean)

    def moe(h3, attn, p):
        return _moe_layer(h3, attn, p, wo, g_ffn1, rtt, m_w1, m_w3, m_w2, wple1, wgate1, g_fin,
                          MOE_CHUNK, MOE_ROWS)

    h1, sre_p, sim_p = _s5_prompt(x_prompt[0].astype(F32), g_mix0, bblk, cre, cim, tabs, d_skip, wa, wb, S5_CHUNK)
    h3, k_p, v_p, q_p, kmean = ffn0(h1, p_prompt[0, 0], True)
    attn_p = _moba_prompt(q_p, k_p, v_p, kmean.reshape(-1, D_MODEL), rel_bias)
    y_p = moe(h3, attn_p, p_prompt[1, 0])

    n_s = n_seq * n_tok
    x_tm = x_sample.astype(F32).transpose(1, 0, 2).reshape(n_s, D_MODEL)
    h1s, sre_s, sim_s = _s5_sample(x_tm, g_mix0, bblk, cre, cim, a1,
                                   state_ssm_re[0].reshape(n_seq, N_STATE), state_ssm_im[0].reshape(n_seq, N_STATE),
                                   d_skip, wa, wb, n_seq, n_tok)
    h1s = h1s.reshape(n_tok, n_seq, D_MODEL).transpose(1, 0, 2).reshape(n_s, D_MODEL)
    h3s, k_s, v_s, q_s = ffn0(h1s, p_sample[0].reshape(n_s, PLE_DIM), False)
    attn_s = _moba_sample(q_s, k_s, v_s, cache_k, cache_v, page_table, rel_bias)
    y_s = moe(h3s, attn_s, p_sample[1].reshape(n_s, PLE_DIM))

    st = lambda s, nb: s.reshape(1, nb, N_GROUPS, SSM_STATE)
    hd = lambda a, *lead: a.reshape(*lead, N_HEADS, HEAD_DIM)
    return (y_p.reshape(1, t_len, D_MODEL), y_s.reshape(n_seq, n_tok, D_MODEL),
            hd(k_p, 1, t_len), hd(v_p, 1, t_len), st(sre_p, 1), st(sim_p, 1),
            hd(k_s, n_seq, n_tok), hd(v_s, n_seq, n_tok), st(sre_s, n_seq), st(sim_s, n_seq))
```

```python
import functools
import math

import jax
import jax.numpy as jnp
import numpy as np
from jax import lax
from jax.experimental import pallas as pl
from jax.experimental.pallas import tpu as pltpu

F32 = jnp.float32
BF16 = jnp.bfloat16

D_MODEL = 1024
SSM_GROUP = 16
N_GROUPS = D_MODEL // SSM_GROUP
SSM_STATE = 64
N_STATE = N_GROUPS * SSM_STATE
S5_BLOCKS = 4
S5_IN = D_MODEL // S5_BLOCKS
S5_ST = N_STATE // S5_BLOCKS
N_HEADS = 16
HEAD_DIM = 64
MOBA_BLOCK = 256
MOBA_TOPK = 3
PAGE_SIZE = 128
PAGES_PER_BLOCK = MOBA_BLOCK // PAGE_SIZE
REL_BUCKETS = 32
REL_MAX_DIST = 128
D_FF = 2816
FF_HALF = D_FF // 2
N_EXPERTS = 8
PLE_DIM = 256
EPS = 1e-6
NEG = -1e30
LOG2E = math.log2(math.e)
SUBLANES = 8
LANES = 128
VMEM_LIMIT = 56 * 1024 * 1024
MOE_VMEM_LIMIT = 60 * 1024 * 1024
PAGES_PER_STEP = 16
FAR_GROUP = 4
DEN_ROWS = 16
SEL_CHUNK = 1024


def _rmsn(x):
    return x * lax.rsqrt(jnp.mean(x * x, axis=-1, keepdims=True) + EPS)


def _dot(a, b):
    return jnp.dot(a, b, preferred_element_type=F32)


def _dot_nt(a, b):
    return lax.dot_general(a, b, (((1,), (1,)), ((), ())), preferred_element_type=F32)


def _split2(x):
    hi = x.astype(BF16)
    lo = (x - hi.astype(F32)).astype(BF16)
    return hi, lo


def _const_spec(shape):
    nd = len(shape)
    return pl.BlockSpec(shape, lambda *_: (0,) * nd, pipeline_mode=pl.Buffered(1))


def _params(sem, vmem=VMEM_LIMIT):
    return pltpu.CompilerParams(dimension_semantics=sem, vmem_limit_bytes=vmem)


def _s5_tables(lam_re, lam_im, log_dt, b_re, b_im, c_re, c_im):
    dt = jnp.exp(log_dt.astype(F32))[:, None]
    lr = lam_re.astype(F32)
    li = lam_im.astype(F32)
    mag = jnp.exp(lr * dt)
    a_re = mag * jnp.cos(li * dt)
    a_im = mag * jnp.sin(li * dt)
    den = lr * lr + li * li
    n_re = a_re - 1.0
    k_re = (n_re * lr + a_im * li) / den
    k_im = (a_im * lr - n_re * li) / den
    bb_re = k_re[..., None] * b_re - k_im[..., None] * b_im
    bb_im = k_re[..., None] * b_im + k_im[..., None] * b_re
    gpb = N_GROUPS // S5_BLOCKS
    eye = jnp.eye(gpb, dtype=F32)

    def pack_b(bb):
        t = bb.reshape(S5_BLOCKS, gpb, SSM_STATE, SSM_GROUP)
        t = jnp.einsum('jgpc,gh->jgchp', t, eye)
        return t.reshape(S5_BLOCKS, S5_IN, S5_ST)

    def pack_c(cc):
        t = cc.astype(F32).reshape(S5_BLOCKS, gpb, SSM_GROUP, SSM_STATE)
        t = jnp.einsum('jgcp,gh->jgphc', t, eye)
        return t.reshape(S5_BLOCKS, S5_ST, S5_IN)

    bblk = jnp.concatenate([pack_b(bb_re), pack_b(bb_im)], axis=-1).astype(BF16)
    cre = pack_c(c_re).astype(BF16)
    cim = (-pack_c(c_im)).astype(BF16)
    ar = a_re.reshape(1, N_STATE)
    ai = a_im.reshape(1, N_STATE)
    pw = [(ar, ai)]
    for _ in range(SUBLANES - 1):
        pr, pi = pw[-1]
        pw.append((pr * ar - pi * ai, pr * ai + pi * ar))
    row = jnp.arange(SUBLANES)[:, None]

    def masked(k):
        return (jnp.where(row >= k, pw[k - 1][0], 0.0), jnp.where(row >= k, pw[k - 1][1], 0.0))

    m1, m2, m4 = masked(1), masked(2), masked(4)
    a_r = jnp.concatenate([p[0] for p in pw], axis=0)
    a_i = jnp.concatenate([p[1] for p in pw], axis=0)
    tabs = jnp.stack([m1[0], m1[1], m2[0], m2[1], m4[0], m4[1], a_r, a_i])
    a1 = jnp.concatenate([ar, ai], axis=0)
    return bblk, cre, cim, tabs, a1


def _s5_out(x, u, y, d_ref, wa_ref, wb_ref):
    y = y + d_ref[...] * u
    hb = jax.nn.gelu(y).astype(BF16)
    return x + _dot(hb, wa_ref[...]) * jax.nn.sigmoid(_dot(hb, wb_ref[...]))


def _s5_prompt_kernel(x_ref, g_ref, bblk_ref, cre_ref, cim_ref, tabs_ref, d_ref, wa_ref, wb_ref,
                      h_ref, sre_ref, sim_ref, cr_s, ci_s, xr_s, xi_s):
    @pl.when(pl.program_id(0) == 0)
    def _():
        cr_s[...] = jnp.zeros_like(cr_s)
        ci_s[...] = jnp.zeros_like(ci_s)

    x = x_ref[...]
    u = _rmsn(x) * g_ref[...]
    ub = u.astype(BF16)
    n8 = x.shape[0] // SUBLANES
    ys = []
    for j in range(S5_BLOCKS):
        lo, hi = j * S5_ST, (j + 1) * S5_ST
        bu = _dot(ub[:, j * S5_IN:(j + 1) * S5_IN], bblk_ref[j])
        xr = bu[:, :S5_ST].reshape(n8, SUBLANES, S5_ST)
        xi = bu[:, S5_ST:].reshape(n8, SUBLANES, S5_ST)
        for k, sh in ((0, 1), (2, 2), (4, 4)):
            mr = tabs_ref[k, :, lo:hi]
            mi = tabs_ref[k + 1, :, lo:hi]
            rr = pltpu.roll(xr, sh, 1)
            ri = pltpu.roll(xi, sh, 1)
            xr, xi = xr + (mr * rr - mi * ri), xi + (mr * ri + mi * rr)
        a_r = tabs_ref[6, :, lo:hi]
        a_i = tabs_ref[7, :, lo:hi]
        pr = cr_s[:, lo:hi]
        pi = ci_s[:, lo:hi]
        for t in range(n8):
            br = jnp.broadcast_to(pr, (SUBLANES, S5_ST))
            bi = jnp.broadcast_to(pi, (SUBLANES, S5_ST))
            tr = xr[t] + (a_r * br - a_i * bi)
            ti = xi[t] + (a_r * bi + a_i * br)
            xr_s[t * SUBLANES:(t + 1) * SUBLANES, :] = tr
            xi_s[t * SUBLANES:(t + 1) * SUBLANES, :] = ti
            pr = tr[SUBLANES - 1:SUBLANES, :]
            pi = ti[SUBLANES - 1:SUBLANES, :]
        cr_s[:, lo:hi] = pr
        ci_s[:, lo:hi] = pi
        ys.append(_dot(xr_s[...].astype(BF16), cre_ref[j]) + _dot(xi_s[...].astype(BF16), cim_ref[j]))
    y = jnp.concatenate(ys, axis=1)
    h_ref[...] = _s5_out(x, u, y, d_ref, wa_ref, wb_ref)
    sre_ref[...] = cr_s[...]
    sim_ref[...] = ci_s[...]


def _s5_prompt(x, g, bblk, cre, cim, tabs, d, wa, wb, chunk):
    t_len = x.shape[0]
    chunk = min(chunk, t_len)
    return pl.pallas_call(
        _s5_prompt_kernel,
        grid=(t_len // chunk,),
        in_specs=[pl.BlockSpec((chunk, D_MODEL), lambda i: (i, 0)),
                  _const_spec(g.shape), _const_spec(bblk.shape), _const_spec(cre.shape), _const_spec(cim.shape),
                  _const_spec(tabs.shape), _const_spec(d.shape), _const_spec(wa.shape), _const_spec(wb.shape)],
        out_specs=[pl.BlockSpec((chunk, D_MODEL), lambda i: (i, 0)),
                   pl.BlockSpec((1, N_STATE), lambda i: (0, 0)),
                   pl.BlockSpec((1, N_STATE), lambda i: (0, 0))],
        out_shape=[jax.ShapeDtypeStruct((t_len, D_MODEL), F32),
                   jax.ShapeDtypeStruct((1, N_STATE), F32),
                   jax.ShapeDtypeStruct((1, N_STATE), F32)],
        scratch_shapes=[pltpu.VMEM((1, N_STATE), F32), pltpu.VMEM((1, N_STATE), F32),
                        pltpu.VMEM((chunk, S5_ST), F32), pltpu.VMEM((chunk, S5_ST), F32)],
        compiler_params=_params(("arbitrary",)),
        name="s5_prompt",
    )(x, g, bblk, cre, cim, tabs, d, wa, wb)


def _s5_sample_kernel(x_ref, g_ref, bblk_ref, cre_ref, cim_ref, a_ref, s0r_ref, s0i_ref, d_ref, wa_ref, wb_ref,
                      h_ref, sre_ref, sim_ref, xr_s, xi_s, *, n_seq, n_tok):
    x = x_ref[...]
    u = _rmsn(x) * g_ref[...]
    ub = u.astype(BF16)
    ys = []
    for j in range(S5_BLOCKS):
        lo, hi = j * S5_ST, (j + 1) * S5_ST
        bu = _dot(ub[:, j * S5_IN:(j + 1) * S5_IN], bblk_ref[j])
        ar = a_ref[0:1, lo:hi]
        ai = a_ref[1:2, lo:hi]
        sr = s0r_ref[:, lo:hi]
        si = s0i_ref[:, lo:hi]
        for t in range(n_tok):
            rows = bu[t * n_seq:(t + 1) * n_seq]
            sr, si = ar * sr - ai * si + rows[:, :S5_ST], ar * si + ai * sr + rows[:, S5_ST:]
            xr_s[t * n_seq:(t + 1) * n_seq, :] = sr
            xi_s[t * n_seq:(t + 1) * n_seq, :] = si
        sre_ref[:, lo:hi] = sr
        sim_ref[:, lo:hi] = si
        ys.append(_dot(xr_s[...].astype(BF16), cre_ref[j]) + _dot(xi_s[...].astype(BF16), cim_ref[j]))
    y = jnp.concatenate(ys, axis=1)
    h_ref[...] = _s5_out(x, u, y, d_ref, wa_ref, wb_ref)


def _s5_sample(x_tm, g, bblk, cre, cim, a1, s0r, s0i, d, wa, wb, n_seq, n_tok):
    n = n_seq * n_tok
    args = (x_tm, g, bblk, cre, cim, a1, s0r, s0i, d, wa, wb)
    return pl.pallas_call(
        functools.partial(_s5_sample_kernel, n_seq=n_seq, n_tok=n_tok),
        grid=(1,),
        in_specs=[_const_spec(a.shape) for a in args],
        out_specs=[_const_spec((n, D_MODEL)), _const_spec((n_seq, N_STATE)), _const_spec((n_seq, N_STATE))],
        out_shape=[jax.ShapeDtypeStruct((n, D_MODEL), F32),
                   jax.ShapeDtypeStruct((n_seq, N_STATE), F32),
                   jax.ShapeDtypeStruct((n_seq, N_STATE), F32)],
        scratch_shapes=[pltpu.VMEM((n, S5_ST), F32), pltpu.VMEM((n, S5_ST), F32)],
        compiler_params=_params(("arbitrary",)),
        name="s5_sample",
    )(*args)


def _ffn0_kernel(h1_ref, p_ref, gf_ref, w1_ref, w3_ref, w2_ref, wple_ref, wgate_ref, gkv_ref, gq_ref,
                 wk_ref, wv_ref, wq_ref, h3_ref, k_ref, v_ref, q_ref, *km_ref, n_blk):
    x = h1_ref[...]
    hn = (_rmsn(x) * gf_ref[...]).astype(BF16)
    h2 = x
    for c in range(2):
        lo, hi = c * FF_HALF, (c + 1) * FF_HALF
        a = _dot(hn, w1_ref[:, lo:hi])
        b = _dot(hn, w3_ref[:, lo:hi])
        h2 = h2 + _dot((jax.nn.silu(a) * b).astype(BF16), w2_ref[lo:hi, :])
    ple = _dot(p_ref[...].astype(BF16), wple_ref[...]) * jax.nn.sigmoid(_dot(h2.astype(BF16), wgate_ref[...]))
    h3 = h2 + ple
    h3_ref[...] = h3
    xn = _rmsn(h3)
    kvb = (xn * gkv_ref[...]).astype(BF16)
    qb = (xn * gq_ref[...]).astype(BF16)
    k = _dot(kvb, wk_ref[...])
    k_ref[...] = k
    v_ref[...] = _dot(kvb, wv_ref[...])
    q_ref[...] = _dot(qb, wq_ref[...])
    for b in range(n_blk):
        km_ref[0][b] = jnp.mean(k[b * MOBA_BLOCK:(b + 1) * MOBA_BLOCK], axis=0, keepdims=True)


def _ffn0(h1, p, gf, w1, w3, w2, wple, wgate, gkv, gq, wk, wv, wq, tm, with_kmean):
    n = h1.shape[0]
    tm = min(tm, n)
    n_blk = tm // MOBA_BLOCK if with_kmean else 0
    consts = (gf, w1, w3, w2, wple, wgate, gkv, gq, wk, wv, wq)
    tile = pl.BlockSpec((tm, D_MODEL), lambda i: (i, 0))
    out_specs = [tile, tile, tile, tile]
    out_shape = [jax.ShapeDtypeStruct((n, D_MODEL), F32)] * 4
    if with_kmean:
        out_specs.append(pl.BlockSpec((n_blk, 1, D_MODEL), lambda i: (i, 0, 0)))
        out_shape.append(jax.ShapeDtypeStruct((n // MOBA_BLOCK, 1, D_MODEL), F32))
    return pl.pallas_call(
        functools.partial(_ffn0_kernel, n_blk=n_blk),
        grid=(n // tm,),
        in_specs=[tile, pl.BlockSpec((tm, PLE_DIM), lambda i: (i, 0))] + [_const_spec(c.shape) for c in consts],
        out_specs=out_specs,
        out_shape=out_shape,
        compiler_params=_params(("arbitrary",)),
        name="ffn0_qkv",
    )(h1, p, *consts)


def _rel_bucket(dist):
    max_exact = REL_BUCKETS // 2
    d = jnp.maximum(dist, 0)
    large = max_exact + (jnp.log(jnp.maximum(d, max_exact).astype(F32) / max_exact)
                         / math.log(REL_MAX_DIST / max_exact) * (REL_BUCKETS - max_exact)).astype(jnp.int32)
    large = jnp.minimum(large, REL_BUCKETS - 1)
    return jnp.where(d < max_exact, d, large)


def _bias_by_distance(rel_bias, n):
    return rel_bias[_rel_bucket(jnp.arange(n))]


FAR_DIST = REL_MAX_DIST


def _top3_rows(cur, rows, n_rows):
    picked = jnp.zeros(cur.shape, jnp.bool_)
    for _ in range(MOBA_TOPK):
        m = jnp.max(cur, axis=0, keepdims=True)
        idx = jnp.min(jnp.where(cur == m, rows, n_rows), axis=0, keepdims=True)
        hit = rows == idx
        picked = jnp.logical_or(picked, hit)
        cur = jnp.where(hit, -jnp.inf, cur)
    return picked


def _moba_prompt_kernel(q_ref, qf_ref, k_ref, v_ref, km_ref, bias_ref, o_ref, sel_s, s_s, smax_s, spair_s, spmax_s,
                        *, n_blk):
    own = pl.program_id(1)
    q = q_ref[0]
    tq = q.shape[0]

    @pl.when(own == 0)
    def _():
        km_hi, km_lo = _split2(km_ref[0])
        chunk = min(SEL_CHUNK, n_blk * MOBA_BLOCK)

        def select(c, _):
            qc = qf_ref[0, pl.ds(pl.multiple_of(c * chunk, chunk), chunk), :]
            q_hi, q_lo = _split2(qc)
            sc = _dot_nt(km_hi, q_hi) + (_dot_nt(km_hi, q_lo) + _dot_nt(km_lo, q_hi))
            rows = lax.broadcasted_iota(jnp.int32, sc.shape, 0)
            cols = lax.broadcasted_iota(jnp.int32, sc.shape, 1) + c * chunk
            past = rows < cols // MOBA_BLOCK
            picked = _top3_rows(jnp.where(past, sc, NEG), rows, n_blk)
            sel = jnp.where(jnp.logical_and(picked, past), 1.0, 0.0)
            for k in range(chunk // MOBA_BLOCK):
                sel_s[c * (chunk // MOBA_BLOCK) + k, 0:n_blk, :] = sel[:, k * MOBA_BLOCK:(k + 1) * MOBA_BLOCK]
            return 0

        lax.fori_loop(0, n_blk * MOBA_BLOCK // chunk, select, 0)

    def sel_rows(start, n):
        base = pl.multiple_of((start // SUBLANES) * SUBLANES, SUBLANES)
        sub = sel_s[own, pl.ds(base, SUBLANES), :]
        off = start - base
        rows8 = lax.broadcasted_iota(jnp.int32, sub.shape, 0)
        return jnp.stack([jnp.sum(jnp.where(rows8 == off + j, sub, 0.0), axis=0, keepdims=True)
                          for j in range(n)])

    qb = (q * (HEAD_DIM ** -0.5 * LOG2E)).astype(BF16)
    grp = FAR_GROUP

    def softmax_step(carry, s3, smax, on3, vs):
        m_run, acc = carry
        m_new = jnp.maximum(m_run, jnp.max(jnp.where(on3, smax, NEG), axis=0))
        alpha = jnp.exp2(m_run - m_new)
        pb = jnp.exp2(s3 - jnp.where(on3, m_new[None], -NEG)).astype(BF16)
        acc = alpha * acc
        for j, vj in enumerate(vs):
            acc = acc + _dot(vj, pb[j])
        return m_new, acc

    n_far = jnp.maximum(own - 1, 0)
    blk_iota = lax.broadcasted_iota(jnp.int32, (grp, 1, tq), 0)

    def produce(g, slot, qx=qb):
        start = jnp.minimum(g * grp, n_blk - grp)
        for j in range(grp):
            s = _dot_nt(k_ref[0, start + j], qx)
            s_s[slot, j] = s
            smax_s[slot, j] = jnp.max(s, axis=0, keepdims=True)

    def produce_pair(tile, qx):
        before = jnp.maximum(tile - 1, 0)
        k2 = jnp.concatenate([k_ref[0, before], k_ref[0, tile]], axis=0)
        s2 = _dot_nt(k2, qx).reshape(2, MOBA_BLOCK, tq) + bias_ref[0]
        spair_s[...] = s2
        spmax_s[...] = jnp.max(s2, axis=1, keepdims=True)

    def consume(g, slot, carry):
        start = jnp.minimum(g * grp, n_blk - grp)
        on3 = jnp.logical_and(sel_rows(start, grp) > 0.0, blk_iota + g * grp < n_far)
        return softmax_step(carry, s_s[slot], smax_s[slot], on3, [v_ref[0, start + j] for j in range(grp)])

    def far_body(i, carry):
        produce(2 * i + 1, 1)
        carry = consume(2 * i, 0, carry)
        produce(2 * i + 2, 0)
        return consume(2 * i + 1, 1, carry)

    @pl.when(own == 0)
    def _():
        produce_pair(own, qb)

    init = (jnp.full((1, tq), NEG, F32), jnp.zeros((HEAD_DIM + DEN_ROWS, tq), F32))
    n_grp = (n_far + grp - 1) // grp
    carry = lax.fori_loop(0, (n_grp + 1) // 2, far_body, init)
    prev = jnp.maximum(own - 1, 0)
    on2 = jnp.concatenate([sel_rows(prev, 1) > 0.0, jnp.ones((1, 1, tq), jnp.bool_)], axis=0)
    _, acc = softmax_step(carry, spair_s[...], spmax_s[...], on2, [v_ref[0, prev], v_ref[0, own]])
    o_ref[0] = acc[:HEAD_DIM] / acc[HEAD_DIM:HEAD_DIM + 1]
    nxt = jnp.minimum(own + 1, n_blk - 1)
    q_nxt = qf_ref[0, pl.ds(pl.multiple_of(nxt * MOBA_BLOCK, MOBA_BLOCK), MOBA_BLOCK), :]
    qb_nxt = (q_nxt * (HEAD_DIM ** -0.5 * LOG2E)).astype(BF16)
    produce(0, 0, qb_nxt)
    produce_pair(nxt, qb_nxt)


def _moba_prompt(q, k, v, kmean, rel_bias):
    t_len = q.shape[0]
    n_blk = t_len // MOBA_BLOCK
    qh = q.reshape(t_len, N_HEADS, HEAD_DIM).transpose(1, 0, 2)
    kh = k.astype(BF16).reshape(n_blk, MOBA_BLOCK, N_HEADS, HEAD_DIM).transpose(2, 0, 1, 3)
    vt = v.astype(BF16).reshape(n_blk, MOBA_BLOCK, N_HEADS, HEAD_DIM).transpose(2, 0, 3, 1)
    vt = jnp.concatenate([vt, jnp.ones((N_HEADS, n_blk, DEN_ROWS, MOBA_BLOCK), BF16)], axis=2)
    kmh = kmean.reshape(n_blk, N_HEADS, HEAD_DIM).transpose(1, 0, 2)
    nb2 = 2 * MOBA_BLOCK
    bd = ((_bias_by_distance(rel_bias, nb2) - rel_bias[REL_BUCKETS - 1]) * LOG2E).T

    def toeplitz(a):
        flat = jnp.tile(a, (1, MOBA_BLOCK))[:, :MOBA_BLOCK * (nb2 - 1)]
        return flat.reshape(N_HEADS, MOBA_BLOCK, nb2 - 1)[:, :, :MOBA_BLOCK]

    own_t = toeplitz(jnp.concatenate([bd[:, :MOBA_BLOCK], jnp.full((N_HEADS, MOBA_BLOCK), NEG, F32)], axis=1))
    prev_t = toeplitz(jnp.concatenate([bd[:, MOBA_BLOCK:], bd[:, :MOBA_BLOCK]], axis=1))
    bias = jnp.stack([prev_t, own_t], axis=1)
    assert n_blk % FAR_GROUP == 0
    out = pl.pallas_call(
        functools.partial(_moba_prompt_kernel, n_blk=n_blk),
        grid=(N_HEADS, n_blk),
        in_specs=[pl.BlockSpec((1, MOBA_BLOCK, HEAD_DIM), lambda h, i: (h, i, 0)),
                  pl.BlockSpec((1, t_len, HEAD_DIM), lambda h, i: (h, 0, 0), pipeline_mode=pl.Buffered(1)),
                  pl.BlockSpec((1, n_blk, MOBA_BLOCK, HEAD_DIM), lambda h, i: (h, 0, 0, 0)),
                  pl.BlockSpec((1, n_blk, HEAD_DIM + DEN_ROWS, MOBA_BLOCK), lambda h, i: (h, 0, 0, 0)),
                  pl.BlockSpec((1, n_blk, HEAD_DIM), lambda h, i: (h, 0, 0)),
                  pl.BlockSpec((1, 2, MOBA_BLOCK, MOBA_BLOCK), lambda h, i: (h, 0, 0, 0))],
        out_specs=pl.BlockSpec((1, HEAD_DIM, MOBA_BLOCK), lambda h, i: (h, 0, i)),
        out_shape=jax.ShapeDtypeStruct((N_HEADS, HEAD_DIM, t_len), F32),
        scratch_shapes=[pltpu.VMEM((n_blk, -(-n_blk // SUBLANES) * SUBLANES, MOBA_BLOCK), F32),
                        pltpu.VMEM((2, FAR_GROUP, MOBA_BLOCK, MOBA_BLOCK), F32),
                        pltpu.VMEM((2, FAR_GROUP, 1, MOBA_BLOCK), F32),
                        pltpu.VMEM((2, MOBA_BLOCK, MOBA_BLOCK), F32),
                        pltpu.VMEM((2, 1, MOBA_BLOCK), F32)],
        compiler_params=_params(("arbitrary", "arbitrary")),
        name="moba_prompt",
    )(qh, qh, kh, vt, kmh, bias)
    return out.transpose(2, 0, 1).reshape(t_len, D_MODEL)


def _page_mean_kernel(pt_ref, *refs):
    del pt_ref
    ones_ref, o_ref = refs[-2], refs[-1]
    for b in range(PAGES_PER_STEP // PAGES_PER_BLOCK):
        x = (refs[2 * b][0] + refs[2 * b + 1][0]).reshape(D_MODEL, PAGE_SIZE)
        hi, lo = _split2(x)
        r = _dot_nt(ones_ref[...], hi) + _dot_nt(ones_ref[...], lo)
        o_ref[b:b + 1, :] = r[0:1] * (1.0 / MOBA_BLOCK)


def _block_means(cache_kt, pt_flat):
    n_pages_total = pt_flat.shape[0]
    steps = n_pages_total // PAGES_PER_STEP
    blocks_per_step = PAGES_PER_STEP // PAGES_PER_BLOCK
    ones = jnp.ones((SUBLANES, PAGE_SIZE), BF16)

    def page_spec(r):
        return pl.BlockSpec((1, N_HEADS, HEAD_DIM, PAGE_SIZE),
                            lambda i, pt: (pt[i * PAGES_PER_STEP + r], 0, 0, 0))

    return pl.pallas_call(
        _page_mean_kernel,
        grid_spec=pltpu.PrefetchScalarGridSpec(
            num_scalar_prefetch=1, grid=(steps,),
            in_specs=[page_spec(r) for r in range(PAGES_PER_STEP)] + [_const_spec(ones.shape)],
            out_specs=pl.BlockSpec((blocks_per_step, D_MODEL), lambda i, pt: (i, 0))),
        out_shape=jax.ShapeDtypeStruct((n_pages_total // PAGES_PER_BLOCK, D_MODEL), F32),
        compiler_params=_params(("arbitrary",)),
        name="page_means",
    )(pt_flat, *([cache_kt] * PAGES_PER_STEP), ones)


def _sample_select_kernel(q_ref, pm_ref, ones_ref, idx_ref, *, n_tok, n_past):
    pm = pm_ref[...]
    blk = lax.broadcasted_iota(jnp.int32, (n_past, N_HEADS, LANES), 0)
    for t in range(n_tok):
        prod = (pm * q_ref[0, t]).reshape(n_past * N_HEADS, HEAD_DIM)
        p1 = prod.astype(BF16)
        r1 = prod - p1.astype(F32)
        p2 = r1.astype(BF16)
        p3 = (r1 - p2.astype(F32)).astype(BF16)
        sc = _dot(p1, ones_ref[...]) + (_dot(p2, ones_ref[...]) + _dot(p3, ones_ref[...]))
        cur = sc.reshape(n_past, N_HEADS, LANES)
        for i in range(MOBA_TOPK):
            m = jnp.max(cur, axis=0, keepdims=True)
            idx = jnp.min(jnp.where(cur == m, blk, n_past), axis=0)
            idx_ref[0, t, i] = idx
            cur = jnp.where(blk == idx[None], -jnp.inf, cur)


def _sample_select(q4, pm, n_past):
    n_seq, n_tok = q4.shape[:2]
    ones = jnp.ones((HEAD_DIM, LANES), BF16)
    return pl.pallas_call(
        functools.partial(_sample_select_kernel, n_tok=n_tok, n_past=n_past),
        grid=(n_seq,),
        in_specs=[pl.BlockSpec((1, n_tok, N_HEADS, HEAD_DIM), lambda b: (b, 0, 0, 0)),
                  pl.BlockSpec((n_past, N_HEADS, HEAD_DIM), lambda b: (b, 0, 0)),
                  _const_spec(ones.shape)],
        out_specs=pl.BlockSpec((1, n_tok, MOBA_TOPK, N_HEADS, LANES), lambda b: (b, 0, 0, 0, 0)),
        out_shape=jax.ShapeDtypeStruct((n_seq, n_tok, MOBA_TOPK, N_HEADS, LANES), jnp.int32),
        compiler_params=_params(("arbitrary",)),
        name="sample_select",
    )(q4, pm, ones)


N_SEL_PAGES = MOBA_TOPK * PAGES_PER_BLOCK


def _sample_attn_kernel(sel_ref, pt_ref, ck_ref, cv_ref, q_ref, kn_ref, vn_ref, near_ref, far_ref, ownb_ref,
                        o_ref, kbuf, vbuf, sems, *, n_tok, n_past, n_pages, n_steps):
    i = pl.program_id(0)
    t = i % n_tok
    slot = i % 2

    def step_copies(step, dst_slot):
        seq = step // n_tok
        cps = []
        for h in range(N_HEADS):
            base = (step * N_HEADS + h) * MOBA_TOPK
            for s in range(N_SEL_PAGES):
                blk = sel_ref[base + s // PAGES_PER_BLOCK]
                page = pt_ref[seq * n_pages + blk * PAGES_PER_BLOCK + s % PAGES_PER_BLOCK]
                lanes = pl.ds(s * PAGE_SIZE, PAGE_SIZE)
                cps.append(pltpu.make_async_copy(ck_ref.at[page, h], kbuf.at[dst_slot, h, :, lanes],
                                                 sems.at[0, dst_slot]))
                cps.append(pltpu.make_async_copy(cv_ref.at[page, h], vbuf.at[dst_slot, h, :, lanes],
                                                 sems.at[1, dst_slot]))
        return cps

    @pl.when(i == 0)
    def _():
        for cp in step_copies(0, 0):
            cp.start()

    @pl.when(i + 1 < n_steps)
    def _():
        for cp in step_copies(i + 1, 1 - slot):
            cp.start()

    for cp in step_copies(i, slot):
        cp.wait()

    qss, lgs = [], []
    for h in range(N_HEADS):
        qs = q_ref[0, 0, h:h + 1, :] * (HEAD_DIM ** -0.5)
        q8 = jnp.broadcast_to(qs, (SUBLANES, HEAD_DIM)).astype(BF16)
        qss.append(qs)
        lgs.append(_dot(q8, kbuf[slot, h].astype(BF16))[0:1])
    p8s, p_owns, dens = [], [], []
    for h in range(N_HEADS):
        base = (i * N_HEADS + h) * MOBA_TOPK
        biases = []
        for s in range(N_SEL_PAGES):
            blk = sel_ref[base + s // PAGES_PER_BLOCK]
            is_near = jnp.logical_and(blk == n_past - 1, (s % PAGES_PER_BLOCK) == PAGES_PER_BLOCK - 1)
            biases.append(jnp.where(is_near, near_ref[t * N_HEADS + h], far_ref[h]))
        lg = lgs[h] + jnp.concatenate(biases, axis=1)
        l_own = jnp.sum(kn_ref[0, :, h, :] * qss[h], axis=1, keepdims=True) + ownb_ref[t * N_HEADS + h]
        m = jnp.maximum(jnp.max(l_own, axis=0, keepdims=True), jnp.max(lg, axis=1, keepdims=True))
        p_own = jnp.exp(l_own - m)
        p = jnp.exp(lg - m)
        dens.append(jnp.sum(p_own, axis=0, keepdims=True) + jnp.sum(p, axis=1, keepdims=True))
        p_owns.append(p_own)
        p8s.append(jnp.broadcast_to(p, (SUBLANES, N_SEL_PAGES * PAGE_SIZE)).astype(BF16))
    for h in range(N_HEADS):
        o = (jnp.sum(p_owns[h] * vn_ref[0, :, h, :], axis=0, keepdims=True)
             + _dot_nt(p8s[h], vbuf[slot, h].astype(BF16))[0:1])
        o_ref[0, 0, h:h + 1, :] = o / dens[h]


def _moba_sample(q, k_new, v_new, cache_k, cache_v, page_table, rel_bias):
    n_seq, n_pages = page_table.shape
    n_tok = q.shape[0] // n_seq
    past_len = n_pages * PAGE_SIZE
    assert past_len % MOBA_BLOCK == 0, "own block must start at the first new token"
    n_past = past_len // MOBA_BLOCK
    assert n_past >= MOBA_TOPK and (n_seq * n_pages) % PAGES_PER_STEP == 0
    pt_flat = page_table.reshape(-1)
    cache_kt = cache_k.transpose(0, 2, 3, 1)
    cache_vt = cache_v.transpose(0, 2, 3, 1)
    pm = _block_means(cache_kt, pt_flat).reshape(-1, N_HEADS, HEAD_DIM)
    q4 = q.reshape(n_seq, n_tok, N_HEADS, HEAD_DIM)
    idx = _sample_select(q4, pm, n_past)
    sel = idx[..., 0].transpose(0, 1, 3, 2).reshape(-1)
    assert PAGE_SIZE + 1 >= FAR_DIST
    bd = _bias_by_distance(rel_bias, PAGE_SIZE + n_tok)
    tt = jnp.arange(n_tok)
    near = bd[PAGE_SIZE + tt[:, None] - jnp.arange(PAGE_SIZE)[None, :]]
    near = near.transpose(0, 2, 1).reshape(n_tok * N_HEADS, 1, PAGE_SIZE)
    far = jnp.broadcast_to(rel_bias[REL_BUCKETS - 1][:, None, None], (N_HEADS, 1, PAGE_SIZE))
    dd = tt[:, None] - tt[None, :]
    ownb = jnp.where((dd >= 0)[..., None], bd[jnp.maximum(dd, 0)], NEG)
    ownb = ownb.transpose(0, 2, 1).reshape(n_tok * N_HEADS, n_tok, 1)
    kn4 = k_new.reshape(n_seq, n_tok, N_HEADS, HEAD_DIM)
    vn4 = v_new.reshape(n_seq, n_tok, N_HEADS, HEAD_DIM)
    n_steps = n_seq * n_tok
    tok_spec = pl.BlockSpec((1, 1, N_HEADS, HEAD_DIM), lambda i, *_: (i // n_tok, i % n_tok, 0, 0))
    new_spec = pl.BlockSpec((1, n_tok, N_HEADS, HEAD_DIM), lambda i, *_: (i // n_tok, 0, 0, 0))
    gathered = pltpu.VMEM((2, N_HEADS, HEAD_DIM, N_SEL_PAGES * PAGE_SIZE), F32)
    out = pl.pallas_call(
        functools.partial(_sample_attn_kernel, n_tok=n_tok, n_past=n_past, n_pages=n_pages, n_steps=n_steps),
        grid_spec=pltpu.PrefetchScalarGridSpec(
            num_scalar_prefetch=2, grid=(n_steps,),
            in_specs=[pl.BlockSpec(memory_space=pl.ANY), pl.BlockSpec(memory_space=pl.ANY),
                      tok_spec, new_spec, new_spec,
                      _const_spec(near.shape), _const_spec(far.shape), _const_spec(ownb.shape)],
            out_specs=tok_spec,
            scratch_shapes=[gathered, gathered, pltpu.SemaphoreType.DMA((2, 2))]),
        out_shape=jax.ShapeDtypeStruct((n_seq, n_tok, N_HEADS, HEAD_DIM), F32),
        compiler_params=_params(("arbitrary",)),
        name="moba_sample",
    )(sel, pt_flat, cache_kt, cache_vt, q4, kn4, vn4, near, far, ownb)
    return out.reshape(n_seq * n_tok, D_MODEL)


def _moe_kernel(h3_ref, at_ref, p_ref, wo_ref, gf_ref, rtt_ref, utri_ref, w1_ref, w3_ref, w2_ref, wple_ref,
                wgate_ref, gfin_ref, y_ref, hn_s, grow_s, rrow_s, rcol_s, cnt_s, x_s, yc_s, *, rows):
    e = pl.program_id(1)
    half = pl.program_id(2)
    tc = hn_s.shape[0]

    slab = min(tc, MOE_SLAB)
    slabs = [slice(s, s + slab) for s in range(0, tc, slab)]

    def stream_in(sl):
        return h3_ref[sl, :] + _dot(at_ref[sl, :].astype(BF16), wo_ref[...])

    @pl.when(jnp.logical_and(e == 0, half == 0))
    def _():
        rt_hi, rt_lo = _split2(rtt_ref[...])
        lts = []
        for sl in slabs:
            hn = _rmsn(stream_in(sl)) * gf_ref[...]
            hn_s[sl, :] = hn.astype(BF16)
            hn_hi, hn_lo = _split2(hn)
            lts.append(_dot_nt(rt_hi, hn_hi) + (_dot_nt(rt_hi, hn_lo) + _dot_nt(rt_lo, hn_hi)))
            y_ref[sl, :] = jnp.zeros((slab, D_MODEL), F32)
        lt = jnp.concatenate(lts, axis=1)
        ex = jnp.exp(lt - jnp.max(lt, axis=0, keepdims=True))
        probs = ex / jnp.sum(ex, axis=0, keepdims=True)
        row = lax.broadcasted_iota(jnp.int32, probs.shape, 0)
        p1 = jnp.max(probs, axis=0, keepdims=True)
        i1 = jnp.min(jnp.where(probs == p1, row, N_EXPERTS), axis=0, keepdims=True)
        rest = jnp.where(row == i1, -1.0, probs)
        p2 = jnp.max(rest, axis=0, keepdims=True)
        i2 = jnp.min(jnp.where(rest == p2, row, N_EXPERTS), axis=0, keepdims=True)
        tot = p1 + p2
        gates = jnp.where(row == i1, p1 / tot, 0.0) + jnp.where(row == i2, p2 / tot, 0.0)
        mem = jnp.where(jnp.logical_or(row == i1, row == i2), 1.0, 0.0)
        rank = jnp.where(mem > 0.0, _dot(mem.astype(BF16), utri_ref[...]), -1.0)
        for k in range(N_EXPERTS):
            grow_s[k] = gates[k:k + 1]
            rrow_s[k] = rank[k:k + 1]
            cnt_s[k] = jnp.sum(mem[k:k + 1]).astype(jnp.int32)
        rcol_s[...] = rank.T

    n_tiles = (cnt_s[e] + rows - 1) // rows

    def one_hot_rows(j):
        r_iota = lax.broadcasted_iota(jnp.int32, (rows, tc), 0) + j * rows
        return jnp.where(rrow_s[e] == r_iota.astype(F32), 1.0, 0.0)

    @pl.when(half == 0)
    def _():
        def compact(j, _):
            x_s[j] = _dot(one_hot_rows(j).astype(BF16), hn_s[...]).astype(BF16)
            return 0
        lax.fori_loop(0, n_tiles, compact, 0)

    def expert_tile(j, _):
        x = x_s[j]
        a = _dot(x, w1_ref[0])
        b = _dot(x, w3_ref[0])
        y = _dot((jax.nn.silu(a) * b).astype(BF16), w2_ref[0])

        @pl.when(half == 0)
        def _():
            yc_s[j] = y

        @pl.when(half == 1)
        def _():
            g_rows = jnp.sum(one_hot_rows(j) * grow_s[e], axis=1, keepdims=True)
            ys = g_rows * (yc_s[j] + y)
            cols = -(-rows // LANES) * LANES
            if cols > rows:
                ys = jnp.concatenate([ys, jnp.zeros((cols - rows, D_MODEL), F32)], axis=0)
            hi, lo = _split2(ys)
            lane_e = lax.broadcasted_iota(jnp.int32, rcol_s.shape, 1)
            rank_col = jnp.sum(jnp.where(lane_e == e, rcol_s[...], 0.0), axis=1, keepdims=True)
            c_iota = lax.broadcasted_iota(jnp.int32, (tc, cols), 1) + j * rows
            scat = jnp.where(rank_col == c_iota.astype(F32), 1.0, 0.0).astype(BF16)
            for sl in slabs:
                y_ref[sl, :] += _dot(scat[sl, :], hi) + _dot(scat[sl, :], lo)
        return 0

    lax.fori_loop(0, n_tiles, expert_tile, 0)

    @pl.when(jnp.logical_and(e == N_EXPERTS - 1, half == 1))
    def _():
        for sl in slabs:
            h5 = stream_in(sl) + y_ref[sl, :]
            ple = (_dot(p_ref[sl, :].astype(BF16), wple_ref[...])
                   * jax.nn.sigmoid(_dot(h5.astype(BF16), wgate_ref[...])))
            y_ref[sl, :] = _rmsn(h5 + ple) * gfin_ref[...]


def _moe_layer(h3, attn, p, wo, gf, rtt, w1, w3, w2, wple, wgate, gfin, chunk, rows):
    n = h3.shape[0]
    tc = min(chunk, n)
    rows = min(rows, tc)
    max_tiles = -(-tc // rows)
    utri = jnp.triu(jnp.ones((tc, tc), BF16), k=1)
    tile = pl.BlockSpec((tc, D_MODEL), lambda i, e, c: (i, 0), pipeline_mode=pl.Buffered(1))
    return pl.pallas_call(
        functools.partial(_moe_kernel, rows=rows),
        grid=(n // tc, N_EXPERTS, 2),
        in_specs=[tile, tile, pl.BlockSpec((tc, PLE_DIM), lambda i, e, c: (i, 0), pipeline_mode=pl.Buffered(1)),
                  _const_spec(wo.shape), _const_spec(gf.shape), _const_spec(rtt.shape), _const_spec(utri.shape),
                  pl.BlockSpec((1, D_MODEL, FF_HALF), lambda i, e, c: (e, 0, c)),
                  pl.BlockSpec((1, D_MODEL, FF_HALF), lambda i, e, c: (e, 0, c)),
                  pl.BlockSpec((1, FF_HALF, D_MODEL), lambda i, e, c: (e, c, 0)),
                  _const_spec(wple.shape), _const_spec(wgate.shape), _const_spec(gfin.shape)],
        out_specs=pl.BlockSpec((tc, D_MODEL), lambda i, e, c: (i, 0)),
        out_shape=jax.ShapeDtypeStruct((n, D_MODEL), F32),
        scratch_shapes=[pltpu.VMEM((tc, D_MODEL), BF16),
                        pltpu.VMEM((N_EXPERTS, 1, tc), F32), pltpu.VMEM((N_EXPERTS, 1, tc), F32),
                        pltpu.VMEM((tc, N_EXPERTS), F32), pltpu.SMEM((N_EXPERTS,), jnp.int32),
                        pltpu.VMEM((max_tiles, rows, D_MODEL), BF16), pltpu.VMEM((max_tiles, rows, D_MODEL), F32)],
        compiler_params=_params(("arbitrary", "arbitrary", "arbitrary"), vmem=MOE_VMEM_LIMIT),
        name="oproj_moe",
    )(h3, attn, p, wo, gf, rtt, utri, w1, w3, w2, wple, wgate, gfin)


S5_CHUNK = 256
TOKEN_TILE = 512
MOE_CHUNK = 1024
MOE_ROWS = 320
MOE_SLAB = 256


def kernel(x_prompt, x_sample, cache_k, cache_v, state_ssm_re, state_ssm_im, page_table, p_prompt, p_sample,
           rel_bias, norm_mix, norm_ffn, norm_kv, norm_final, ssm_lam_re, ssm_lam_im, ssm_log_dt, ssm_b_re,
           ssm_b_im, ssm_c_re, ssm_c_im, ssm_d, w_glu_a, w_glu_b, w_q, w_k, w_v, w_o, ffn_w1, ffn_w3, ffn_w2,
           router, moe_w1, moe_w3, moe_w2, w_ple, w_ple_gate):
    assert x_prompt.shape[0] == 1 and norm_mix.shape[0] == 2, "one prompt sequence, depth 2"
    t_len = x_prompt.shape[1]
    n_seq, n_tok, _ = x_sample.shape
    row = lambda v: v.reshape(1, -1).astype(F32)
    bf = lambda w: w.astype(BF16)

    bblk, cre, cim, tabs, a1 = _s5_tables(ssm_lam_re[0], ssm_lam_im[0], ssm_log_dt[0], ssm_b_re[0], ssm_b_im[0],
                                          ssm_c_re[0], ssm_c_im[0])
    g_mix0, g_mix1 = row(norm_mix[0]), row(norm_mix[1])
    g_ffn0, g_ffn1 = row(norm_ffn[0]), row(norm_ffn[1])
    g_kv, g_fin, d_skip = row(norm_kv), row(norm_final), row(ssm_d[0])
    wa, wb = bf(w_glu_a[0]), bf(w_glu_b[0])
    f_w1, f_w3, f_w2 = bf(ffn_w1[0]), bf(ffn_w3[0]), bf(ffn_w2[0])
    wple0, wple1 = bf(w_ple[0]), bf(w_ple[1])
    wgate0, wgate1 = bf(w_ple_gate[0]), bf(w_ple_gate[1])
    wk, wv, wq, wo = bf(w_k), bf(w_v), bf(w_q[0]), bf(w_o[0])
    m_w1, m_w3, m_w2 = bf(moe_w1[0]), bf(moe_w3[0]), bf(moe_w2[0])
    rtt = router[0].astype(F32).T

    def ffn0(h1, p, with_kmean):
        return _ffn0(h1, p, g_ffn0, f_w1, f_w3, f_w2, wple0, wgate0, g_kv, g_mix1, wk, wv, wq,
                     TOKEN_TILE, with_kmean)

    def moe(h3, attn, p):
        return _moe_layer(h3, attn, p, wo, g_ffn1, rtt, m_w1, m_w3, m_w2, wple1, wgate1, g_fin,
                          MOE_CHUNK, MOE_ROWS)

    h1, sre_p, sim_p = _s5_prompt(x_prompt[0].astype(F32), g_mix0, bblk, cre, cim, tabs, d_skip, wa, wb, S5_CHUNK)
    h3, k_p, v_p, q_p, kmean = ffn0(h1, p_prompt[0, 0], True)
    attn_p = _moba_prompt(q_p, k_p, v_p, kmean.reshape(-1, D_MODEL), rel_bias)
    y_p = moe(h3, attn_p, p_prompt[1, 0])

    n_s = n_seq * n_tok
    x_tm = x_sample.astype(F32).transpose(1, 0, 2).reshape(n_s, D_MODEL)
    h1s, sre_s, sim_s = _s5_sample(x_tm, g_mix0, bblk, cre, cim, a1,
                                   state_ssm_re[0].reshape(n_seq, N_STATE), state_ssm_im[0].reshape(n_seq, N_STATE),
                                   d_skip, wa, wb, n_seq, n_tok)
    h1s = h1s.reshape(n_tok, n_seq, D_MODEL).transpose(1, 0, 2).reshape(n_s, D_MODEL)
    h3s, k_s, v_s, q_s = ffn0(h1s, p_sample[0].reshape(n_s, PLE_DIM), False)
    attn_s = _moba_sample(q_s, k_s, v_s, cache_k, cache_v, page_table, rel_bias)
    y_s = moe(h3s, attn_s, p_sample[1].reshape(n_s, PLE_DIM))

    st = lambda s, nb: s.reshape(1, nb, N_GROUPS, SSM_STATE)
    hd = lambda a, *lead: a.reshape(*lead, N_HEADS, HEAD_DIM)
    return (y_p.reshape(1, t_len, D_MODEL), y_s.reshape(n_seq, n_tok, D_MODEL),
            hd(k_p, 1, t_len), hd(v_p, 1, t_len), st(sre_p, 1), st(sim_p, 1),
            hd(k_s, n_seq, n_tok), hd(v_s, n_seq, n_tok), st(sre_s, n_seq), st(sim_s, n_seq))
```
